```python
import math
import jax, jax.numpy as jnp
from jax import lax
import numpy as np

D_MODEL = 2048
BATCH = 8
SEQ = 2048
DEPTH = 1
DEC_BATCH = 32
DEC_SEQ = 4
PAST_LEN = 16384
PAGE_SIZE = 128

ATTN_HEADS = 8
ATTN_HEAD_DIM = 128
ATTN_WIDTH = ATTN_HEADS * ATTN_HEAD_DIM
MOBA_BLOCK = 256
MOBA_TOPK = 3
Q_CHUNK = 32
SSM_WIDTH = D_MODEL // 2
SSM_GROUP = 16
SSM_GROUPS = SSM_WIDTH // SSM_GROUP
SSM_STATE = 64
N_EXPERTS = 32
TOP_K = 4
D_FF = D_MODEL
SWIGLU_ALPHA = 1.702
SWIGLU_LIMIT = 7.0
MOE_ROWS = 256
RMS_EPS = 1e-5
N_MOD = 6
IN_WIDTH = 3 * ATTN_WIDTH + SSM_WIDTH + 2 * D_MODEL
IN_SPLITS = (ATTN_WIDTH, 2 * ATTN_WIDTH, 3 * ATTN_WIDTH, 3 * ATTN_WIDTH + SSM_WIDTH,
             3 * ATTN_WIDTH + SSM_WIDTH + D_MODEL)

kernel_name = 'moba_s5_gated_moe_decode_step'


def rmsnorm(x, g):
    xf = x.astype(jnp.float32)
    y = xf * lax.rsqrt(jnp.mean(xf * xf, axis=-1, keepdims=True) + RMS_EPS)
    return (y * g.astype(jnp.float32)).astype(x.dtype)


def moba_sequence(q, k, v):
    T = q.shape[0]
    L = k.shape[0]
    nb = -(-L // MOBA_BLOCK)
    pad = nb * MOBA_BLOCK - L
    kb = jnp.pad(k, ((0, pad), (0, 0), (0, 0))).reshape(nb, MOBA_BLOCK, ATTN_HEADS, ATTN_HEAD_DIM)
    vb = jnp.pad(v, ((0, pad), (0, 0), (0, 0))).reshape(nb, MOBA_BLOCK, ATTN_HEADS, ATTN_HEAD_DIM)
    kb = kb.transpose(0, 2, 1, 3)
    vb = vb.transpose(0, 2, 1, 3)
    kmean = jnp.mean(kb.astype(jnp.float32), axis=2)
    pos = (L - T) + jnp.arange(T)
    own = pos // MOBA_BLOCK
    qf = q.astype(jnp.float32)
    gate = jnp.einsum('thd,nhd->thn', qf, kmean)
    past_ok = jnp.arange(nb)[None, :] < own[:, None]
    past_ok3 = jnp.broadcast_to(past_ok[:, None, :], gate.shape)
    gate = jnp.where(past_ok3, gate, -jnp.inf)
    ksel = min(MOBA_TOPK, nb)
    _, sel = lax.top_k(gate, ksel)
    sel_ok = jnp.take_along_axis(past_ok3, sel, axis=-1)
    blocks = jnp.concatenate([sel, jnp.broadcast_to(own[:, None, None], (T, ATTN_HEADS, 1))], axis=-1)
    blk_ok = jnp.concatenate([sel_ok, jnp.ones((T, ATTN_HEADS, 1), bool)], axis=-1)
    J = ksel + 1
    qc = math.gcd(T, Q_CHUNK)
    nc = T // qc
    scale = ATTN_HEAD_DIM ** -0.5
    hidx = jnp.arange(ATTN_HEADS)[None, :, None]

    def chunk(args):
        q_c, b_c, ok_c, p_c = args
        kg = kb[b_c, hidx].astype(jnp.float32)
        vg = vb[b_c, hidx].astype(jnp.float32)
        s = jnp.einsum('qhd,qhjpd->qhjp', q_c, kg) * scale
        kpos = b_c[..., None] * MOBA_BLOCK + jnp.arange(MOBA_BLOCK)
        mask = ok_c[..., None] & (kpos <= p_c[:, None, None, None])
        s = jnp.where(mask, s, -jnp.inf)
        w = jax.nn.softmax(s.reshape(qc, ATTN_HEADS, -1), axis=-1).reshape(s.shape)
        return jnp.einsum('qhjp,qhjpd->qhd', w, vg)

    out = lax.map(chunk, (qf.reshape(nc, qc, ATTN_HEADS, ATTN_HEAD_DIM),
                          blocks.reshape(nc, qc, ATTN_HEADS, J),
                          blk_ok.reshape(nc, qc, ATTN_HEADS, J),
                          pos.reshape(nc, qc)))
    return out.reshape(T, ATTN_HEADS, ATTN_HEAD_DIM).astype(q.dtype)


def s5_scan(u, s0_re, s0_im, a_re, a_im, log_dt, b_re, b_im, c_re, c_im, d_skip):
    f32 = jnp.float32
    a_re = a_re.astype(f32); a_im = a_im.astype(f32)
    b_re = b_re.astype(f32); b_im = b_im.astype(f32)
    dt = jnp.exp(log_dt.astype(f32))[:, None]
    mag = jnp.exp(a_re * dt)
    abar_re = mag * jnp.cos(a_im * dt)
    abar_im = mag * jnp.sin(a_im * dt)
    den = a_re * a_re + a_im * a_im
    nr = abar_re - 1.0
    ni = abar_im
    coef_re = ((nr * a_re + ni * a_im) / den)[..., None]
    coef_im = ((ni * a_re - nr * a_im) / den)[..., None]
    bb_re = coef_re * b_re - coef_im * b_im
    bb_im = coef_re * b_im + coef_im * b_re
    uf = u.astype(f32)
    x_re = jnp.einsum('gpc,btgc->btgp', bb_re, uf)
    x_im = jnp.einsum('gpc,btgc->btgp', bb_im, uf)
    s0_re = s0_re.astype(f32); s0_im = s0_im.astype(f32)
    x_re = x_re.at[:, 0].add(abar_re * s0_re - abar_im * s0_im)
    x_im = x_im.at[:, 0].add(abar_re * s0_im + abar_im * s0_re)
    ar_t = jnp.broadcast_to(abar_re, x_re.shape)
    ai_t = jnp.broadcast_to(abar_im, x_im.shape)

    def combine(e1, e2):
        a1r, a1i, b1r, b1i = e1
        a2r, a2i, b2r, b2i = e2
        return (a1r * a2r - a1i * a2i, a1r * a2i + a1i * a2r,
                a2r * b1r - a2i * b1i + b2r, a2r * b1i + a2i * b1r + b2i)

    _, _, s_re, s_im = lax.associative_scan(combine, (ar_t, ai_t, x_re, x_im), axis=1)
    y = (jnp.einsum('gcp,btgp->btgc', c_re.astype(f32), s_re)
         - jnp.einsum('gcp,btgp->btgc', c_im.astype(f32), s_im)
         + d_skip.astype(f32) * uf)
    return y, s_re[:, -1], s_im[:, -1]


def moe(u, w_router, b_router, w_gate_up, b_gate_up, w_down, b_down):
    bt, t, d = u.shape
    xf = u.reshape(-1, d)
    n = xf.shape[0]
    logits = (xf @ w_router + b_router).astype(jnp.float32)
    top_v, top_i = lax.top_k(logits, TOP_K)
    gates = jax.nn.softmax(top_v, axis=-1)
    nk = n * TOP_K
    expert_of = top_i.reshape(-1)
    token_of = jnp.repeat(jnp.arange(n), TOP_K)
    order = jnp.argsort(expert_of)
    e_sorted = expert_of[order]
    counts = jnp.bincount(expert_of, length=N_EXPERTS)
    starts = jnp.cumsum(counts) - counts
    padded = (counts + MOE_ROWS - 1) // MOE_ROWS * MOE_ROWS
    pad_end = jnp.cumsum(padded)
    pad_start = pad_end - padded
    dest = pad_start[e_sorted] + jnp.arange(nk) - starts[e_sorted]
    n_blocks = -(-nk // MOE_ROWS) + N_EXPERTS
    tok_sorted = token_of[order]
    rows = jnp.zeros((n_blocks * MOE_ROWS, d), xf.dtype).at[dest].set(xf[tok_sorted])
    block_expert = jnp.minimum(
        jnp.searchsorted(pad_end, jnp.arange(n_blocks) * MOE_ROWS, side='right'), N_EXPERTS - 1)

    def expert_block(args):
        xb, e = args
        hgu = xb @ w_gate_up[e] + b_gate_up[e]
        x_glu = jnp.minimum(hgu[:, ::2], SWIGLU_LIMIT)
        x_lin = jnp.clip(hgu[:, 1::2], -SWIGLU_LIMIT, SWIGLU_LIMIT)
        act = x_glu * jax.nn.sigmoid(SWIGLU_ALPHA * x_glu) * (x_lin + 1.0)
        return act @ w_down[e] + b_down[e]

    out_rows = lax.map(expert_block, (rows.reshape(n_blocks, MOE_ROWS, d), block_expert))
    out_rows = out_rows.reshape(-1, d)
    contrib = out_rows[dest] * gates.reshape(-1)[order][:, None].astype(out_rows.dtype)
    y = jax.ops.segment_sum(contrib, tok_sorted, num_segments=n)
    return y.reshape(bt, t, d)


def setup_inputs(seed: int = 0) -> dict:
    key = jax.random.key(seed)
    ks = iter(jax.random.split(key, 40))
    f32 = jnp.float32

    def nrm(shape, s):
        return jax.random.normal(next(ks), shape, f32) * s

    n_pages = PAST_LEN // PAGE_SIZE
    used = DEC_BATCH * n_pages
    n_pool = used + max(1, used // 4)
    perm = jax.random.permutation(next(ks), n_pool)
    page_table = perm[:used].reshape(DEC_BATCH, n_pages).astype(jnp.int32)
    n_idx = jnp.arange(SSM_STATE, dtype=f32)
    kv_shape = (DEPTH, n_pool, PAGE_SIZE, ATTN_HEADS, ATTN_HEAD_DIM)
    st_shape = (DEPTH, DEC_BATCH, SSM_GROUPS, SSM_STATE)
    return {
        'x_prompt': nrm((BATCH, SEQ, D_MODEL), 1.0),
        'x_sample': nrm((DEC_BATCH, DEC_SEQ, D_MODEL), 1.0),
        'cache_k': nrm(kv_shape, 1.0),
        'cache_v': nrm(kv_shape, 1.0),
        'state_ssm_re': nrm(st_shape, 0.5),
        'state_ssm_im': nrm(st_shape, 0.5),
        'page_table': page_table,
        'c_prompt': nrm((BATCH, D_MODEL), 1.0),
        'c_sample': nrm((DEC_BATCH, D_MODEL), 1.0),
        'w_ada': nrm((DEPTH, D_MODEL, N_MOD * D_MODEL), D_MODEL ** -0.5),
        'b_ada': nrm((DEPTH, N_MOD * D_MODEL), 0.02),
        'norm1_g': 1.0 + nrm((DEPTH, D_MODEL), 0.02),
        'w_in': nrm((DEPTH, D_MODEL, IN_WIDTH), D_MODEL ** -0.5),
        'w_attn_br': nrm((DEPTH, ATTN_WIDTH, D_MODEL), ATTN_WIDTH ** -0.5),
        'ssm_a_re': -0.5 + nrm((DEPTH, SSM_GROUPS, SSM_STATE), 0.01),
        'ssm_a_im': math.pi * n_idx + nrm((DEPTH, SSM_GROUPS, SSM_STATE), 0.01),
        'ssm_log_dt': jax.random.uniform(next(ks), (DEPTH, SSM_GROUPS), f32,
                                         minval=math.log(1e-3), maxval=math.log(1e-1)),
        'ssm_b_re': nrm((DEPTH, SSM_GROUPS, SSM_STATE, SSM_GROUP), (2 * SSM_GROUP) ** -0.5),
        'ssm_b_im': nrm((DEPTH, SSM_GROUPS, SSM_STATE, SSM_GROUP), (2 * SSM_GROUP) ** -0.5),
        'ssm_c_re': nrm((DEPTH, SSM_GROUPS, SSM_GROUP, SSM_STATE), (2 * SSM_STATE) ** -0.5),
        'ssm_c_im': nrm((DEPTH, SSM_GROUPS, SSM_GROUP, SSM_STATE), (2 * SSM_STATE) ** -0.5),
        'ssm_d': nrm((DEPTH, SSM_GROUPS, SSM_GROUP), 1.0),
        'w_ssm_val': nrm((DEPTH, SSM_WIDTH, D_MODEL), SSM_WIDTH ** -0.5),
        'w_ssm_gate': nrm((DEPTH, SSM_WIDTH, D_MODEL), SSM_WIDTH ** -0.5),
        'w_out': nrm((DEPTH, D_MODEL, D_MODEL), D_MODEL ** -0.5),
        'norm2_g': 1.0 + nrm((DEPTH, D_MODEL), 0.02),
        'w_router': nrm((DEPTH, D_MODEL, N_EXPERTS), D_MODEL ** -0.5),
        'b_router': nrm((DEPTH, N_EXPERTS), 0.01),
        'w_gate_up': nrm((DEPTH, N_EXPERTS, D_MODEL, 2 * D_FF), D_MODEL ** -0.5),
        'b_gate_up': nrm((DEPTH, N_EXPERTS, 2 * D_FF), 0.02),
        'w_down': nrm((DEPTH, N_EXPERTS, D_FF, D_MODEL), D_FF ** -0.5),
        'b_down': nrm((DEPTH, N_EXPERTS, D_MODEL), 0.02),
        'final_g': 1.0 + nrm((D_MODEL,), 0.02),
    }


def reference(x_prompt, x_sample, cache_k, cache_v, state_ssm_re, state_ssm_im, page_table,
              c_prompt, c_sample, w_ada, b_ada, norm1_g, w_in, w_attn_br, ssm_a_re, ssm_a_im,
              ssm_log_dt, ssm_b_re, ssm_b_im, ssm_c_re, ssm_c_im, ssm_d, w_ssm_val, w_ssm_gate,
              w_out, norm2_g, w_router, b_router, w_gate_up, b_gate_up, w_down, b_down, final_g):
    H, DH = ATTN_HEADS, ATTN_HEAD_DIM

    def attend_prompt(q, k, v, l):
        return lax.map(lambda a: moba_sequence(a[0], a[1], a[2]), (q, k, v))

    def attend_sample(q, k, v, l):
        ck = cache_k[l]
        cv = cache_v[l]

        def one(a):
            q_b, k_b, v_b, pt = a
            k_all = jnp.concatenate([ck[pt].reshape(-1, H, DH).astype(k_b.dtype), k_b], axis=0)
            v_all = jnp.concatenate([cv[pt].reshape(-1, H, DH).astype(v_b.dtype), v_b], axis=0)
            return moba_sequence(q_b, k_all, v_all)

        return lax.map(one, (q, k, v, page_table))

    def layer(x, c, l, attend, s0_re, s0_im):
        bt, t, _ = x.shape
        mod = (jax.nn.silu(c) @ w_ada[l] + b_ada[l])[:, None, :]
        sh1, sc1, g1, sh2, sc2, g2 = jnp.split(mod, N_MOD, axis=-1)
        u = rmsnorm(x, norm1_g[l]) * (1.0 + sc1) + sh1
        q, k, v, xs, ga, gb = jnp.split(u @ w_in[l], IN_SPLITS, axis=-1)
        q = q.reshape(bt, t, H, DH)
        k = k.reshape(bt, t, H, DH)
        v = v.reshape(bt, t, H, DH)
        att = attend(q, k, v, l).reshape(bt, t, ATTN_WIDTH)
        y_att = att @ w_attn_br[l]
        ys, s_re, s_im = s5_scan(xs.reshape(bt, t, SSM_GROUPS, SSM_GROUP), s0_re, s0_im,
                                 ssm_a_re[l], ssm_a_im[l], ssm_log_dt[l], ssm_b_re[l], ssm_b_im[l],
                                 ssm_c_re[l], ssm_c_im[l], ssm_d[l])
        gy = jax.nn.gelu(ys.reshape(bt, t, SSM_WIDTH))
        y_ssm = (gy @ w_ssm_val[l]) * jax.nn.sigmoid(gy @ w_ssm_gate[l])
        merged = jax.nn.sigmoid(ga) * y_att + jax.nn.sigmoid(gb) * y_ssm
        h = x + g1 * (merged @ w_out[l])
        u2 = rmsnorm(h, norm2_g[l]) * (1.0 + sc2) + sh2
        h = h + g2 * moe(u2, w_router[l], b_router[l], w_gate_up[l], b_gate_up[l], w_down[l], b_down[l])
        return h, k, v, s_re, s_im

    zeros_state = jnp.zeros((x_prompt.shape[0], SSM_GROUPS, SSM_STATE), jnp.float32)
    hp, hs = x_prompt, x_sample
    kp_l, vp_l, srp_l, sip_l, ks_l, vs_l, srs_l, sis_l = [], [], [], [], [], [], [], []
    for l in range(DEPTH):
        hp, kp, vp, srp, sip = layer(hp, c_prompt, l, attend_prompt, zeros_state, zeros_state)
        hs, ks_, vs_, srs, sis = layer(hs, c_sample, l, attend_sample, state_ssm_re[l], state_ssm_im[l])
        kp_l.append(kp); vp_l.append(vp); srp_l.append(srp); sip_l.append(sip)
        ks_l.append(ks_); vs_l.append(vs_); srs_l.append(srs); sis_l.append(sis)
    y_prompt = rmsnorm(hp, final_g)
    y_sample = rmsnorm(hs, final_g)
    return (y_prompt, y_sample,
            jnp.stack(kp_l), jnp.stack(vp_l), jnp.stack(srp_l), jnp.stack(sip_l),
            jnp.stack(ks_l), jnp.stack(vs_l), jnp.stack(srs_l), jnp.stack(sis_l))
```

```python
import functools
import math

import jax
import jax.numpy as jnp
from jax import lax
from jax.experimental import pallas as pl
from jax.experimental.pallas import tpu as pltpu

F32 = jnp.float32
BF16 = jnp.bfloat16
I32 = jnp.int32
HIGHEST = lax.Precision.HIGHEST

MOBA_BLOCK = 256
MOBA_TOPK = 3
SSM_GROUP = 16
MOE_TOPK = 4
SWIGLU_ALPHA = 1.702
SWIGLU_LIMIT = 7.0
RMS_EPS = 1e-5
N_MOD = 6

LANES = 128
SUBLANES = 8
VMEM_LIMIT_BYTES = 56 * 1024 * 1024

ADA_TN = 1024
PROJ_TN = 1024
PROJ_TM = 512
SCAN_TC = 64
SCAN_SLAB = 256
SCAN_COLS = 512
RANK_TM = 512
MOE_TM = 512
MOE_TF = 512
MIX_TM = 256
DISPATCH_TM = 256
COMBINE_TM = 128

NEG = -1e30


def _cparams(sem, vmem=VMEM_LIMIT_BYTES):
    return pltpu.CompilerParams(dimension_semantics=sem, vmem_limit_bytes=vmem)


def _dot(a, b):
    return jnp.dot(a, b, preferred_element_type=F32)


def _dot_nt(a, b, precision=None):
    return lax.dot_general(a, b, (((1,), (1,)), ((), ())), precision=precision,
                           preferred_element_type=F32)


def _sigmoid(x):
    return 1.0 / (1.0 + jnp.exp(-x))


def _rmsnorm(x, g):
    return (x * lax.rsqrt(jnp.mean(x * x, axis=-1, keepdims=True) + RMS_EPS)) * g


def _ada_kernel(c_ref, w_ref, b_ref, o_ref):
    c = c_ref[...]
    s = (c * _sigmoid(c)).astype(BF16)
    o_ref[...] = _dot(s, w_ref[...].astype(BF16)) + b_ref[...]


def _ada(c, w, b):
    r, d = c.shape
    m = w.shape[1]
    return pl.pallas_call(
        _ada_kernel,
        grid=(m // ADA_TN,),
        in_specs=[pl.BlockSpec((r, d), lambda j: (0, 0)),
                  pl.BlockSpec((d, ADA_TN), lambda j: (0, j)),
                  pl.BlockSpec((1, ADA_TN), lambda j: (0, j))],
        out_specs=pl.BlockSpec((r, ADA_TN), lambda j: (0, j)),
        out_shape=jax.ShapeDtypeStruct((r, m), F32),
        compiler_params=_cparams(("parallel",)),
        name="ada_mod",
    )(c, w, b)


def _inproj_kernel(x_ref, sc_ref, sh_ref, g_ref, w_ref,
                   q_ref, k_ref, v_ref, xs_ref, ga_ref, gb_ref, u_scr):
    j = pl.program_id(1)

    @pl.when(j == 0)
    def _():
        u = _rmsnorm(x_ref[...], g_ref[...]) * (1.0 + sc_ref[0]) + sh_ref[0]
        u_scr[...] = u.astype(BF16)

    r = _dot(u_scr[...], w_ref[...])
    for jj, ref in ((0, q_ref), (1, k_ref), (2, v_ref)):
        @pl.when(j == jj)
        def _(ref=ref):
            ref[...] = r

    @pl.when(j == 3)
    def _():
        xs_ref[...] = r.astype(BF16)

    @pl.when((j == 4) | (j == 5))
    def _():
        ga_ref[...] = r.astype(BF16)

    @pl.when(j >= 6)
    def _():
        gb_ref[...] = r.astype(BF16)


def _inproj(x, sc, sh, g, w_bf, tm, tiles_per_group):
    n, d = x.shape
    tn = PROJ_TN
    assert w_bf.shape[1] == 8 * tn and d == 2 * tn, "projection layout: q|k|v|ssm each one tile, gates two"
    r = sc.shape[1]
    mod_spec = pl.BlockSpec((1, r, d), lambda i, j: (i // tiles_per_group, 0, 0))
    row = lambda i, j: (i, 0)
    outs = (
        jax.ShapeDtypeStruct((n, tn), F32),
        jax.ShapeDtypeStruct((n, tn), F32),
        jax.ShapeDtypeStruct((n, tn), F32),
        jax.ShapeDtypeStruct((n, tn), BF16),
        jax.ShapeDtypeStruct((n, 2 * tn), BF16),
        jax.ShapeDtypeStruct((n, 2 * tn), BF16),
    )
    return pl.pallas_call(
        _inproj_kernel,
        grid=(n // tm, 8),
        in_specs=[pl.BlockSpec((tm, d), row), mod_spec, mod_spec,
                  pl.BlockSpec((1, d), lambda i, j: (0, 0)),
                  pl.BlockSpec((d, tn), lambda i, j: (0, j))],
        out_specs=(pl.BlockSpec((tm, tn), row), pl.BlockSpec((tm, tn), row),
                   pl.BlockSpec((tm, tn), row), pl.BlockSpec((tm, tn), row),
                   pl.BlockSpec((tm, tn), lambda i, j: (i, jnp.clip(j - 4, 0, 1))),
                   pl.BlockSpec((tm, tn), lambda i, j: (i, jnp.clip(j - 6, 0, 1)))),
        out_shape=outs,
        scratch_shapes=[pltpu.VMEM((tm, d), BF16)],
        compiler_params=_cparams(("parallel", "arbitrary")),
        name="in_proj",
    )(x, sc, sh, g, w_bf)


def _moba_prompt_kernel(q_ref, k_ref, v_ref, o_ref, kb_scr, vt_scr, km_scr, *, nb, scale):
    qi = pl.program_id(2)
    blk = MOBA_BLOCK
    dh = q_ref.shape[-1]

    @pl.when(qi == 0)
    def _():
        k = k_ref[...]
        kb_scr[...] = k.astype(BF16)
        km_scr[...] = jnp.sum(k.reshape(nb, blk, dh), axis=1) * (1.0 / blk)
        vt_scr[...] = v_ref[...].T.astype(BF16)

    q = q_ref[...]
    gate_t = _dot_nt(km_scr[...], q, precision=HIGHEST)
    jidx = lax.broadcasted_iota(I32, gate_t.shape, 0)
    cnt = jnp.zeros(gate_t.shape, I32)
    for jp in range(nb):
        row = gate_t[jp:jp + 1, :]
        beats = (row > gate_t) | ((row == gate_t) & (jp < jidx))
        cnt = cnt + jnp.where(beats & (jp < qi), 1, 0)
    sel = jnp.where(((cnt < MOBA_TOPK) & (jidx < qi)) | (jidx == qi), 1.0, 0.0)

    s_t = _dot_nt(kb_scr[...], q.astype(BF16)) * scale
    diff = (lax.broadcasted_iota(I32, (blk, blk), 0) - lax.broadcasted_iota(I32, (blk, blk), 1))
    blocks = []
    m = jnp.full((1, blk), NEG, F32)
    for j in range(nb):
        sj = s_t[j * blk:(j + 1) * blk, :]
        ok = (sel[j:j + 1, :] > 0.5) & (diff <= (qi - j) * blk)
        sj = jnp.where(ok, sj, NEG)
        blocks.append(sj)
        m = jnp.maximum(m, jnp.max(sj, axis=0, keepdims=True))
    l = jnp.zeros((1, blk), F32)
    acc = jnp.zeros((dh, blk), F32)
    for j in range(nb):
        p = jnp.exp(blocks[j] - m)
        l = l + jnp.sum(p, axis=0, keepdims=True)
        acc = acc + _dot(vt_scr[:, j * blk:(j + 1) * blk], p.astype(BF16))
    o_ref[...] = (acc / l).T.astype(o_ref.dtype)


def _moba_prompt(q, k, v, n_heads):
    b, t, w = q.shape
    dh = w // n_heads
    assert t % MOBA_BLOCK == 0 and dh == LANES
    nb = t // MOBA_BLOCK
    kern = functools.partial(_moba_prompt_kernel, nb=nb, scale=dh ** -0.5)
    kv_spec = pl.BlockSpec((None, t, dh), lambda bi, h, qi: (bi, 0, h))
    q_spec = pl.BlockSpec((None, MOBA_BLOCK, dh), lambda bi, h, qi: (bi, qi, h))
    return pl.pallas_call(
        kern,
        grid=(b, n_heads, nb),
        in_specs=[q_spec, kv_spec, kv_spec],
        out_specs=q_spec,
        out_shape=jax.ShapeDtypeStruct((b, t, w), BF16),
        scratch_shapes=[pltpu.VMEM((t, dh), BF16), pltpu.VMEM((dh, t), BF16),
                        pltpu.VMEM((nb, dh), F32)],
        compiler_params=_cparams(("parallel", "parallel", "arbitrary")),
        name="moba_prompt",
    )(q, k, v)


def _kmean_kernel(pt_ref, *refs, ppb):
    del pt_ref
    o_ref = refs[ppb]
    j = pl.program_id(1)
    s = jnp.sum(refs[0][...], axis=0)
    for p in range(1, ppb):
        s = s + jnp.sum(refs[p][...], axis=0)
    o_ref[pl.ds(j, 1)] = (s * (1.0 / MOBA_BLOCK))[None]


def _cache_block_means(cache_k, layer, page_table, n_blocks):
    _, _, page, h, dh = cache_k.shape
    db = page_table.shape[0]
    ppb = MOBA_BLOCK // page
    page_specs = [
        pl.BlockSpec((None, None, page, h, dh),
                     functools.partial(lambda b, j, pt, p: (layer, pt[b, j * ppb + p], 0, 0, 0), p=p))
        for p in range(ppb)]
    return pl.pallas_call(
        functools.partial(_kmean_kernel, ppb=ppb),
        grid_spec=pltpu.PrefetchScalarGridSpec(
            num_scalar_prefetch=1,
            grid=(db, n_blocks),
            in_specs=page_specs,
            out_specs=pl.BlockSpec((None, n_blocks, h, dh), lambda b, j, pt: (b, 0, 0, 0))),
        out_shape=jax.ShapeDtypeStruct((db, n_blocks, h, dh), F32),
        compiler_params=_cparams(("parallel", "arbitrary")),
        name="cache_block_means",
    )(page_table, *([cache_k] * ppb))


def _select_kernel(q_ref, km_ref, o_ref, *, n_heads, n_blocks):
    dh = km_ref.shape[-1]
    rows = []
    for h in range(n_heads):
        qh = q_ref[:, h * dh:(h + 1) * dh]
        rows.append(_dot_nt(qh, km_ref[:, h, :], precision=HIGHEST))
    g = jnp.concatenate(rows, axis=0)
    lane = lax.broadcasted_iota(I32, g.shape, 1)
    out_lane = lax.broadcasted_iota(I32, o_ref.shape, 1)
    out = jnp.zeros(o_ref.shape, I32)
    for t in range(MOBA_TOPK):
        m = jnp.max(g, axis=1, keepdims=True)
        idx = jnp.min(jnp.where(g == m, lane, n_blocks), axis=1, keepdims=True)
        out = jnp.where(out_lane == t, idx, out)
        g = jnp.where(lane == idx, -jnp.inf, g)
    o_ref[...] = out


def _select_blocks(q_pad, kmean, n_heads):
    db, qp, w = q_pad.shape
    n_blocks = kmean.shape[1]
    dh = w // n_heads
    return pl.pallas_call(
        functools.partial(_select_kernel, n_heads=n_heads, n_blocks=n_blocks),
        grid=(db,),
        in_specs=[pl.BlockSpec((None, qp, w), lambda b: (b, 0, 0)),
                  pl.BlockSpec((None, n_blocks, n_heads, dh), lambda b: (b, 0, 0, 0))],
        out_specs=pl.BlockSpec((None, n_heads * qp, LANES), lambda b: (b, 0, 0)),
        out_shape=jax.ShapeDtypeStruct((db, n_heads * qp, LANES), I32),
        compiler_params=_cparams(("parallel",)),
        name="moba_select",
    )(q_pad, kmean)


def _sample_attn_kernel(sel_ref, pt_ref, q_ref, kn_ref, vn_ref, ck_hbm, cv_hbm, o_ref,
                        kbuf, vbuf, sems, *, layer, n_heads, n_q, qp, page, scale):
    b = pl.program_id(0)
    ppb = MOBA_BLOCK // page
    dh = kbuf.shape[-1]
    per_head = n_q * MOBA_TOPK * ppb
    span = MOBA_TOPK * MOBA_BLOCK

    def copies(i):
        h = i // per_head
        rem = i % per_head
        qs = rem // ppb
        p = rem % ppb
        q = qs // MOBA_TOPK
        s = qs % MOBA_TOPK
        blk = sel_ref[b, (h * qp + q) * MOBA_TOPK + s]
        pg = pt_ref[b, blk * ppb + p]
        dst = pl.ds(qs * MOBA_BLOCK + p * page, page)
        return (pltpu.make_async_copy(ck_hbm.at[layer, pg, :, h, :], kbuf.at[h, dst, :], sems.at[0]),
                pltpu.make_async_copy(cv_hbm.at[layer, pg, :, h, :], vbuf.at[h, dst, :], sems.at[1]))

    def start(i, c):
        ck, cv = copies(i)
        ck.start()
        cv.start()
        return c

    def wait(i, c):
        ck, cv = copies(i)
        ck.wait()
        cv.wait()
        return c

    lax.fori_loop(0, n_heads * per_head, start, 0)
    lax.fori_loop(0, n_heads * per_head, wait, 0)

    n_keys = n_q * span
    row = lax.broadcasted_iota(I32, (qp, n_keys), 0)
    col = lax.broadcasted_iota(I32, (qp, n_keys), 1)
    own = (col >= row * span) & (col < (row + 1) * span)
    rown = lax.broadcasted_iota(I32, (qp, qp), 0)
    coln = lax.broadcasted_iota(I32, (qp, qp), 1)
    causal = (coln <= rown) & (coln < n_q)
    for h in range(n_heads):
        hs = slice(h * dh, (h + 1) * dh)
        qh = q_ref[:, hs].astype(BF16)
        s = jnp.where(own, _dot_nt(qh, kbuf[h].astype(BF16)) * scale, NEG)
        sn = jnp.where(causal, _dot_nt(qh, kn_ref[:, hs].astype(BF16)) * scale, NEG)
        m = jnp.maximum(jnp.max(s, axis=1, keepdims=True), jnp.max(sn, axis=1, keepdims=True))
        p = jnp.exp(s - m)
        pn = jnp.exp(sn - m)
        l = jnp.sum(p, axis=1, keepdims=True) + jnp.sum(pn, axis=1, keepdims=True)
        o = _dot(p.astype(BF16), vbuf[h].astype(BF16)) + _dot(pn.astype(BF16), vn_ref[:, hs].astype(BF16))
        o_ref[:, hs] = o / l


def _sample_attention(q_pad, k_pad, v_pad, sel_flat, page_table, cache_k, cache_v, layer, n_heads, n_q):
    db, qp, w = q_pad.shape
    dh = w // n_heads
    page = cache_k.shape[2]
    n_keys = n_q * MOBA_TOPK * MOBA_BLOCK
    kern = functools.partial(_sample_attn_kernel, layer=layer, n_heads=n_heads, n_q=n_q, qp=qp,
                             page=page, scale=dh ** -0.5)
    row_spec = pl.BlockSpec((None, qp, w), lambda b, sel, pt: (b, 0, 0))
    return pl.pallas_call(
        kern,
        grid_spec=pltpu.PrefetchScalarGridSpec(
            num_scalar_prefetch=2,
            grid=(db,),
            in_specs=[row_spec, row_spec, row_spec,
                      pl.BlockSpec(memory_space=pl.ANY), pl.BlockSpec(memory_space=pl.ANY)],
            out_specs=row_spec,
            scratch_shapes=[pltpu.VMEM((n_heads, n_keys, dh), F32),
                            pltpu.VMEM((n_heads, n_keys, dh), F32),
                            pltpu.SemaphoreType.DMA((2,))]),
        out_shape=jax.ShapeDtypeStruct((db, qp, w), F32),
        compiler_params=_cparams(("arbitrary",)),
        name="moba_sample",
    )(sel_flat, page_table, q_pad, k_pad, v_pad, cache_k, cache_v)


def _s5_param_kernel(are_ref, aim_ref, ldt_ref, bre_ref, bim_ref,
                     o_are, o_aim, o_bre, o_bim):
    a_re = are_ref[...]
    a_im = aim_ref[...]
    dt = jnp.exp(ldt_ref[...])
    mag = jnp.exp(a_re * dt)
    ab_re = mag * jnp.cos(a_im * dt)
    ab_im = mag * jnp.sin(a_im * dt)
    den = a_re * a_re + a_im * a_im
    nr = ab_re - 1.0
    ni = ab_im
    c_re = (nr * a_re + ni * a_im) / den
    c_im = (ni * a_re - nr * a_im) / den
    b_re = bre_ref[...]
    b_im = bim_ref[...]
    o_are[...] = ab_re
    o_aim[...] = ab_im
    o_bre[...] = c_re * b_re - c_im * b_im
    o_bim[...] = c_re * b_im + c_im * b_re


def _s5_params(a_re, a_im, log_dt, b_re, b_im):
    g, p = a_re.shape
    c = b_re.shape[-1]
    bt_re = jnp.swapaxes(b_re, 1, 2)
    bt_im = jnp.swapaxes(b_im, 1, 2)
    sds = jax.ShapeDtypeStruct
    ab_re, ab_im, bb_re, bb_im = pl.pallas_call(
        _s5_param_kernel,
        out_shape=(sds((g, 1, p), F32), sds((g, 1, p), F32), sds((g, c, p), F32), sds((g, c, p), F32)),
        name="s5_params",
    )(a_re.reshape(g, 1, p), a_im.reshape(g, 1, p), log_dt.reshape(g, 1, 1), bt_re, bt_im)
    return ab_re.reshape(g, p), ab_im.reshape(g, p), bb_re, bb_im


def _block_diag_slabs(m, groups_per_slab):
    g, r, c = m.shape
    n = g // groups_per_slab
    eye = jnp.eye(groups_per_slab, dtype=m.dtype)
    m4 = m.reshape(n, groups_per_slab, r, c)
    out = jnp.einsum("ngrc,gh->ngrhc", m4, eye)
    return out.reshape(n, groups_per_slab * r, groups_per_slab * c)


def _s5_kernel(xs_ref, s0re_ref, s0im_ref, are_ref, aim_ref, bre_ref, bim_ref, cre_ref, cim_ref, d_ref,
               gy_ref, fre_ref, fim_ref, sre, sim, st_re, st_im, *, n_batch, rows_per_batch):
    c = pl.program_id(0)
    n_slab, slab_in, slab_st = bre_ref.shape
    mm_groups, mm_rows, _ = xs_ref.shape
    n_state = sre.shape[0] * LANES
    rpb = rows_per_batch
    tiles_per_slab = slab_st // LANES
    tiles_per_pass = SCAN_COLS // LANES

    @pl.when(c == 0)
    def _():
        st_re[...] = s0re_ref[...]
        st_im[...] = s0im_ref[...]

    for g in range(mm_groups):
        rows = slice(g * mm_rows, (g + 1) * mm_rows)
        for k in range(n_slab):
            u = xs_ref[g, :, k * slab_in:(k + 1) * slab_in]
            x_re = _dot(u, bre_ref[k])
            x_im = _dot(u, bim_ref[k])
            for ct in range(tiles_per_slab):
                lanes = slice(ct * LANES, (ct + 1) * LANES)
                sre[k * tiles_per_slab + ct, rows, :] = x_re[:, lanes]
                sim[k * tiles_per_slab + ct, rows, :] = x_im[:, lanes]

    for bg in range(n_batch // SUBLANES):
        brows = slice(bg * SUBLANES, (bg + 1) * SUBLANES)
        for cs in range(n_state // SCAN_COLS):
            tiles = range(cs * tiles_per_pass, (cs + 1) * tiles_per_pass)
            lanes = [slice(ct * LANES, (ct + 1) * LANES) for ct in tiles]
            a_re = [jnp.broadcast_to(are_ref[:, ls], (SUBLANES, LANES)) for ls in lanes]
            a_im = [jnp.broadcast_to(aim_ref[:, ls], (SUBLANES, LANES)) for ls in lanes]

            def step(t, carry, tiles=tiles, bg=bg, a_re=a_re, a_im=a_im):
                rows = pl.ds(bg * SUBLANES * rpb + t, SUBLANES, stride=rpb)
                out = []
                for n, ct in enumerate(tiles):
                    s_re, s_im = carry[2 * n], carry[2 * n + 1]
                    n_re = a_re[n] * s_re - a_im[n] * s_im + sre[ct, rows, :]
                    n_im = a_re[n] * s_im + a_im[n] * s_re + sim[ct, rows, :]
                    sre[ct, rows, :] = n_re
                    sim[ct, rows, :] = n_im
                    out += [n_re, n_im]
                return tuple(out)

            init = []
            for ls in lanes:
                init += [st_re[brows, ls], st_im[brows, ls]]
            fin = lax.fori_loop(0, rpb, step, tuple(init))
            for n, ls in enumerate(lanes):
                st_re[brows, ls] = fin[2 * n]
                st_im[brows, ls] = fin[2 * n + 1]

    fre_ref[...] = st_re[...]
    fim_ref[...] = st_im[...]

    for g in range(mm_groups):
        rows = slice(g * mm_rows, (g + 1) * mm_rows)
        for k in range(n_slab):
            tiles = range(k * tiles_per_slab, (k + 1) * tiles_per_slab)
            ucols = slice(k * slab_in, (k + 1) * slab_in)
            s_re = jnp.concatenate([sre[ct, rows, :] for ct in tiles], axis=1).astype(BF16)
            s_im = jnp.concatenate([sim[ct, rows, :] for ct in tiles], axis=1).astype(BF16)
            y = (_dot(s_re, cre_ref[k]) - _dot(s_im, cim_ref[k])
                 + d_ref[:, ucols] * xs_ref[g, :, ucols].astype(F32))
            inner = math.sqrt(2.0 / math.pi) * (y + 0.044715 * (y * y * y))
            gy_ref[g, :, ucols] = (0.5 * y * (1.0 + jnp.tanh(inner))).astype(gy_ref.dtype)


def _s5(xs3, s0_re, s0_im, abar_re, abar_im, bre_bd, bim_bd, cre_bd, cim_bd, d_row, n_batch, rows_per_batch):
    mg, t, w = xs3.shape
    mm_rows = rows_per_batch if mg == n_batch else t
    n_state = abar_re.shape[1]
    n_chunks = t // mm_rows
    rows = mg * mm_rows
    assert rows == n_batch * rows_per_batch and n_batch % SUBLANES == 0 and n_state % SCAN_COLS == 0
    kern = functools.partial(_s5_kernel, n_batch=n_batch, rows_per_batch=rows_per_batch)
    full = lambda a: pl.BlockSpec(a.shape, lambda c: (0,) * a.ndim)
    st_spec = pl.BlockSpec((n_batch, n_state), lambda c: (0, 0))
    x_spec = pl.BlockSpec((mg, mm_rows, w), lambda c: (0, c, 0))
    sds = jax.ShapeDtypeStruct
    return pl.pallas_call(
        kern,
        grid=(n_chunks,),
        in_specs=[x_spec, st_spec, st_spec, full(abar_re), full(abar_im), full(bre_bd), full(bim_bd),
                  full(cre_bd), full(cim_bd), full(d_row)],
        out_specs=(x_spec, st_spec, st_spec),
        out_shape=(sds((mg, t, w), BF16), sds((n_batch, n_state), F32), sds((n_batch, n_state), F32)),
        scratch_shapes=[pltpu.VMEM((n_state // LANES, rows, LANES), F32),
                        pltpu.VMEM((n_state // LANES, rows, LANES), F32),
                        pltpu.VMEM((n_batch, n_state), F32), pltpu.VMEM((n_batch, n_state), F32)],
        compiler_params=_cparams(("arbitrary",)),
        name="s5_scan",
    )(xs3, s0_re, s0_im, abar_re, abar_im, bre_bd, bim_bd, cre_bd, cim_bd, d_row)


def _merge_kernel(att_ref, gy_ref, ga_ref, gb_ref, wbr_ref, wv_ref, wg_ref, o_ref):
    y_att = _dot(att_ref[...], wbr_ref[...])
    gy = gy_ref[...]
    y_ssm = _dot(gy, wv_ref[...]) * _sigmoid(_dot(gy, wg_ref[...]))
    merged = _sigmoid(ga_ref[...].astype(F32)) * y_att + _sigmoid(gb_ref[...].astype(F32)) * y_ssm
    o_ref[...] = merged.astype(o_ref.dtype)


def _merge(att, gy, ga, gb, wbr, wv, wg, tm):
    n, aw = att.shape
    sw = gy.shape[1]
    d = ga.shape[1]
    row = lambda i: (i, 0)
    const = lambda i: (0, 0)
    return pl.pallas_call(
        _merge_kernel,
        grid=(n // tm,),
        in_specs=[pl.BlockSpec((tm, aw), row), pl.BlockSpec((tm, sw), row),
                  pl.BlockSpec((tm, d), row), pl.BlockSpec((tm, d), row),
                  pl.BlockSpec((aw, d), const), pl.BlockSpec((sw, d), const), pl.BlockSpec((sw, d), const)],
        out_specs=pl.BlockSpec((tm, d), row),
        out_shape=jax.ShapeDtypeStruct((n, d), BF16),
        compiler_params=_cparams(("parallel",)),
        name="branch_merge",
    )(att, gy, ga, gb, wbr, wv, wg)


def _outproj_kernel(m_ref, x_ref, g1_ref, sc_ref, sh_ref, ng_ref, wo_ref, wr_ref, br_ref,
                    h_ref, u_ref, ti_ref, gt_ref):
    h = x_ref[...] + g1_ref[0] * _dot(m_ref[...], wo_ref[...])
    h_ref[...] = h
    u = _rmsnorm(h, ng_ref[...]) * (1.0 + sc_ref[0]) + sh_ref[0]
    u_ref[...] = u
    logits = _dot_nt(wr_ref[...], u, precision=HIGHEST) + br_ref[...]
    n_exp = logits.shape[0]
    eidx = lax.broadcasted_iota(I32, logits.shape, 0)
    vals, idxs = [], []
    for _ in range(MOE_TOPK):
        m = jnp.max(logits, axis=0, keepdims=True)
        idx = jnp.min(jnp.where(logits == m, eidx, n_exp), axis=0, keepdims=True)
        vals.append(m)
        idxs.append(idx)
        logits = jnp.where(eidx == idx, -jnp.inf, logits)
    es = [jnp.exp(v - vals[0]) for v in vals]
    tot = es[0]
    for e in es[1:]:
        tot = tot + e
    ti_ref[...] = jnp.concatenate(idxs, axis=0)
    gt_ref[...] = jnp.concatenate([e / tot for e in es], axis=0)


def _outproj(merged, x, g1, sc, sh, ng, wo_bf, wr_t, br_col, tm, tiles_per_group):
    n, d = x.shape
    n_exp = wr_t.shape[0]
    r = sc.shape[1]
    mod_spec = pl.BlockSpec((1, r, d), lambda i: (i // tiles_per_group, 0, 0))
    row = lambda i: (i, 0)
    const = lambda i: (0, 0)
    sds = jax.ShapeDtypeStruct
    return pl.pallas_call(
        _outproj_kernel,
        grid=(n // tm,),
        in_specs=[pl.BlockSpec((tm, d), row), pl.BlockSpec((tm, d), row), mod_spec, mod_spec, mod_spec,
                  pl.BlockSpec((1, d), const), pl.BlockSpec((d, d), const),
                  pl.BlockSpec((n_exp, d), const), pl.BlockSpec((n_exp, 1), const)],
        out_specs=(pl.BlockSpec((tm, d), row), pl.BlockSpec((tm, d), row),
                   pl.BlockSpec((MOE_TOPK, tm), lambda i: (0, i)),
                   pl.BlockSpec((MOE_TOPK, tm), lambda i: (0, i))),
        out_shape=(sds((n, d), F32), sds((n, d), F32), sds((MOE_TOPK, n), I32), sds((MOE_TOPK, n), F32)),
        compiler_params=_cparams(("parallel",)),
        name="out_proj_router",
    )(merged, x, g1, sc, sh, ng, wo_bf, wr_t, br_col)


def _rank_kernel(ti_ref, rank_ref, cnt_ref, carry, *, n_exp):
    i = pl.program_id(0)
    tm = ti_ref.shape[1]

    @pl.when(i == 0)
    def _():
        carry[...] = jnp.zeros(carry.shape, F32)

    ti = ti_ref[...]
    eidx = lax.broadcasted_iota(I32, (n_exp, tm), 0)
    tri = jnp.where(lax.broadcasted_iota(I32, (tm, tm), 0) <= lax.broadcasted_iota(I32, (tm, tm), 1),
                    1.0, 0.0).astype(BF16)
    base = carry[...]
    ranks = []
    for k in range(MOE_TOPK):
        hit = eidx == ti[k:k + 1, :]
        onehot = jnp.where(hit, 1.0, 0.0)
        incl = _dot(onehot.astype(BF16), tri)
        ranks.append(jnp.sum(jnp.where(hit, base + incl - 1.0, 0.0), axis=0, keepdims=True))
        base = base + jnp.sum(onehot, axis=1, keepdims=True)
    carry[...] = base
    rank_ref[...] = jnp.concatenate(ranks, axis=0).astype(I32)
    cnt_ref[...] = jnp.broadcast_to(base, cnt_ref.shape)


def _expert_ranks(top_i, n_exp):
    k, n = top_i.shape
    rank, cnt = pl.pallas_call(
        functools.partial(_rank_kernel, n_exp=n_exp),
        grid=(n // RANK_TM,),
        in_specs=[pl.BlockSpec((k, RANK_TM), lambda i: (0, i))],
        out_specs=(pl.BlockSpec((k, RANK_TM), lambda i: (0, i)),
                   pl.BlockSpec((n_exp, LANES), lambda i: (0, 0))),
        out_shape=(jax.ShapeDtypeStruct((k, n), I32), jax.ShapeDtypeStruct((n_exp, LANES), F32)),
        scratch_shapes=[pltpu.VMEM((n_exp, 1), F32)],
        compiler_params=_cparams(("arbitrary",)),
        name="expert_ranks",
    )(top_i)
    return rank, cnt[:, 0].astype(I32)


def _dispatch_kernel(dest_ref, u_ref, rows_in, rows_out, sem):
    del rows_in
    i = pl.program_id(0)
    tm = u_ref.shape[0]
    base = i * tm * MOE_TOPK

    def copy(t, k, d):
        return pltpu.make_async_copy(u_ref.at[pl.ds(t, 1), :], rows_out.at[pl.ds(d, 1), :], sem)

    def start(t, c):
        for k in range(MOE_TOPK):
            copy(t, k, dest_ref[base + t * MOE_TOPK + k]).start()
        return c

    def wait(t, c):
        for k in range(MOE_TOPK):
            copy(0, k, 0).wait()
        return c

    lax.fori_loop(0, tm, start, 0)
    lax.fori_loop(0, tm, wait, 0)


def _dispatch(dest_flat, u, rows, tm):
    n, d = u.shape
    return pl.pallas_call(
        _dispatch_kernel,
        grid_spec=pltpu.PrefetchScalarGridSpec(
            num_scalar_prefetch=1,
            grid=(n // tm,),
            in_specs=[pl.BlockSpec((tm, d), lambda i, dest: (i, 0)),
                      pl.BlockSpec(memory_space=pl.ANY)],
            out_specs=pl.BlockSpec(memory_space=pl.ANY),
            scratch_shapes=[pltpu.SemaphoreType.DMA(())]),
        out_shape=jax.ShapeDtypeStruct(rows.shape, rows.dtype),
        input_output_aliases={2: 0},
        compiler_params=_cparams(("arbitrary",)),
        name="moe_dispatch",
    )(dest_flat, u, rows)


def _expert_kernel(be_ref, nu_ref, x_ref, wg_ref, wl_ref, wd_ref, bg_ref, bl_ref, bd_ref, o_ref, xb_scr):
    del be_ref
    i = pl.program_id(0)
    f = pl.program_id(1)

    @pl.when(i < nu_ref[0])
    def _():
        @pl.when(f == 0)
        def _():
            xb_scr[...] = x_ref[...].astype(BF16)

        xb = xb_scr[...]
        h_glu = jnp.minimum(_dot(xb, wg_ref[...]) + bg_ref[...], SWIGLU_LIMIT)
        h_lin = jnp.clip(_dot(xb, wl_ref[...]) + bl_ref[...], -SWIGLU_LIMIT, SWIGLU_LIMIT)
        act = h_glu * _sigmoid(SWIGLU_ALPHA * h_glu) * (h_lin + 1.0)
        part = _dot(act.astype(BF16), wd_ref[...])

        @pl.when(f == 0)
        def _():
            o_ref[...] = part + bd_ref[...]

        @pl.when(f > 0)
        def _():
            o_ref[...] += part

    @pl.when((i >= nu_ref[0]) & (f == 0))
    def _():
        o_ref[...] = jnp.zeros(o_ref.shape, o_ref.dtype)


def _experts(rows, block_expert, n_used, w_glu, w_lin, w_down, b_glu, b_lin, b_down):
    r, d = rows.shape
    n_exp, _, ff = w_glu.shape
    tm, tf = MOE_TM, MOE_TF
    nf = ff // tf
    nblk = r // tm

    def blk(i, nu):
        return jnp.minimum(i, nu[0] - 1)

    def fcol(i, f, nu):
        return jnp.where(i < nu[0], f, nf - 1)

    return pl.pallas_call(
        _expert_kernel,
        grid_spec=pltpu.PrefetchScalarGridSpec(
            num_scalar_prefetch=2,
            grid=(nblk, nf),
            in_specs=[
                pl.BlockSpec((tm, d), lambda i, f, be, nu: (blk(i, nu), 0)),
                pl.BlockSpec((None, d, tf), lambda i, f, be, nu: (be[blk(i, nu)], 0, fcol(i, f, nu))),
                pl.BlockSpec((None, d, tf), lambda i, f, be, nu: (be[blk(i, nu)], 0, fcol(i, f, nu))),
                pl.BlockSpec((None, tf, d), lambda i, f, be, nu: (be[blk(i, nu)], fcol(i, f, nu), 0)),
                pl.BlockSpec((None, 1, tf), lambda i, f, be, nu: (be[blk(i, nu)], 0, fcol(i, f, nu))),
                pl.BlockSpec((None, 1, tf), lambda i, f, be, nu: (be[blk(i, nu)], 0, fcol(i, f, nu))),
                pl.BlockSpec((None, 1, d), lambda i, f, be, nu: (be[blk(i, nu)], 0, 0)),
            ],
            out_specs=pl.BlockSpec((tm, d), lambda i, f, be, nu: (i, 0)),
            scratch_shapes=[pltpu.VMEM((tm, d), BF16)]),
        out_shape=jax.ShapeDtypeStruct((r, d), F32),
        compiler_params=_cparams(("arbitrary", "arbitrary")),
        name="moe_experts",
    )(block_expert, n_used, rows, w_glu, w_lin, w_down, b_glu, b_lin, b_down)


def _combine_kernel(dest_ref, gt_ref, h_ref, g2_ref, fg_ref, orow_hbm, y_ref, gath, sem):
    i = pl.program_id(0)
    tm = h_ref.shape[0]
    base = i * tm * MOE_TOPK

    def copy(t, k, d):
        return pltpu.make_async_copy(orow_hbm.at[pl.ds(d, 1), :], gath.at[k, pl.ds(t, 1), :], sem)

    def start(t, c):
        for k in range(MOE_TOPK):
            copy(t, k, dest_ref[base + t * MOE_TOPK + k]).start()
        return c

    def wait(t, c):
        for k in range(MOE_TOPK):
            copy(0, k, 0).wait()
        return c

    lax.fori_loop(0, tm, start, 0)
    lax.fori_loop(0, tm, wait, 0)

    gt = gt_ref[...]
    y = gt[:, 0:1] * gath[0]
    for k in range(1, MOE_TOPK):
        y = y + gt[:, k:k + 1] * gath[k]
    h = h_ref[...] + g2_ref[0] * y
    y_ref[...] = _rmsnorm(h, fg_ref[...])


def _combine(dest_flat, gates_t, h, g2, final_g, out_rows, tm, tiles_per_group):
    n, d = h.shape
    r = g2.shape[1]
    row = lambda i, dest: (i, 0)
    return pl.pallas_call(
        _combine_kernel,
        grid_spec=pltpu.PrefetchScalarGridSpec(
            num_scalar_prefetch=1,
            grid=(n // tm,),
            in_specs=[pl.BlockSpec((tm, MOE_TOPK), row), pl.BlockSpec((tm, d), row),
                      pl.BlockSpec((1, r, d), lambda i, dest: (i // tiles_per_group, 0, 0)),
                      pl.BlockSpec((1, d), lambda i, dest: (0, 0)),
                      pl.BlockSpec(memory_space=pl.ANY)],
            out_specs=pl.BlockSpec((tm, d), row),
            scratch_shapes=[pltpu.VMEM((MOE_TOPK, tm, d), F32), pltpu.SemaphoreType.DMA(())]),
        out_shape=jax.ShapeDtypeStruct((n, d), F32),
        compiler_params=_cparams(("arbitrary",)),
        name="moe_combine",
    )(dest_flat, gates_t, h, g2, final_g, out_rows)


def _mods(mod, rows_per_batch, per_row):
    b = mod.shape[0]
    parts = jnp.split(mod, N_MOD, axis=-1)
    if per_row:
        return [jnp.repeat(p, rows_per_batch, axis=0)[None] for p in parts]
    return [p.reshape(b, 1, -1) for p in parts]


def kernel(x_prompt, x_sample, cache_k, cache_v, state_ssm_re, state_ssm_im, page_table, c_prompt, c_sample,
           w_ada, b_ada, norm1_g, w_in, w_attn_br, ssm_a_re, ssm_a_im, ssm_log_dt, ssm_b_re, ssm_b_im,
           ssm_c_re, ssm_c_im, ssm_d, w_ssm_val, w_ssm_gate, w_out, norm2_g, w_router, b_router,
           w_gate_up, b_gate_up, w_down, b_down, final_g):
    depth = w_in.shape[0]
    bp, t, d = x_prompt.shape
    db, tq, _ = x_sample.shape
    _, n_pool, page, n_heads, dh = cache_k.shape
    aw = n_heads * dh
    n_groups, n_st = ssm_a_re.shape[1:]
    sw = n_groups * SSM_GROUP
    n_state = n_groups * n_st
    n_exp = w_router.shape[-1]
    past = page_table.shape[1] * page
    n_past_blocks = past // MOBA_BLOCK
    assert past % MOBA_BLOCK == 0 and MOBA_BLOCK % page == 0, "cached keys must fill whole MoBA blocks"
    assert n_past_blocks >= MOBA_TOPK and tq <= SUBLANES, "sample step: all selected blocks are past blocks"
    assert sw % SCAN_SLAB == 0 and t % SCAN_TC == 0 and t % PROJ_TM == 0

    np_tok = bp * t
    ns_tok = db * tq
    n_tok = np_tok + ns_tok
    qp = SUBLANES

    hp = x_prompt.reshape(np_tok, d)
    hs = x_sample.reshape(ns_tok, d)
    c_all = jnp.concatenate([c_prompt, c_sample], axis=0)

    outs = {k: [] for k in ("kp", "vp", "srp", "sip", "ks", "vs", "srs", "sis")}
    gps = SCAN_SLAB // SSM_GROUP
    for l in range(depth):
        mod = _ada(c_all, w_ada[l], b_ada[l][None])
        mod_p = _mods(mod[:bp], t, per_row=False)
        mod_s = _mods(mod[bp:], tq, per_row=True)

        w_in_bf = w_in[l].astype(BF16)
        wbr = w_attn_br[l].astype(BF16)
        wv = w_ssm_val[l].astype(BF16)
        wg = w_ssm_gate[l].astype(BF16)
        wo = w_out[l].astype(BF16)
        wr_t = w_router[l].T
        br_col = b_router[l][:, None]
        n1g = norm1_g[l][None]
        n2g = norm2_g[l][None]

        abar_re, abar_im, bbt_re, bbt_im = _s5_params(ssm_a_re[l], ssm_a_im[l], ssm_log_dt[l],
                                                      ssm_b_re[l], ssm_b_im[l])
        bre_bd = _block_diag_slabs(bbt_re, gps).astype(BF16)
        bim_bd = _block_diag_slabs(bbt_im, gps).astype(BF16)
        cre_bd = _block_diag_slabs(jnp.swapaxes(ssm_c_re[l], 1, 2), gps).astype(BF16)
        cim_bd = _block_diag_slabs(jnp.swapaxes(ssm_c_im[l], 1, 2), gps).astype(BF16)
        abar_re_row = abar_re.reshape(1, n_state)
        abar_im_row = abar_im.reshape(1, n_state)
        d_row = ssm_d[l].reshape(1, sw)

        tpg = t // PROJ_TM
        q, k, v, xs, ga, gb = _inproj(hp, mod_p[1], mod_p[0], n1g, w_in_bf, PROJ_TM, tpg)
        att = _moba_prompt(q.reshape(bp, t, aw), k.reshape(bp, t, aw), v.reshape(bp, t, aw), n_heads)
        zeros_state = jnp.zeros((bp, n_state), F32)
        gy, srp, sip = _s5(xs.reshape(bp, t, sw), zeros_state, zeros_state, abar_re_row, abar_im_row,
                           bre_bd, bim_bd, cre_bd, cim_bd, d_row, n_batch=bp, rows_per_batch=SCAN_TC)
        merged = _merge(att.reshape(np_tok, aw), gy.reshape(np_tok, sw), ga, gb, wbr, wv, wg, MIX_TM)
        hp_mid, u2p, tip, gtp = _outproj(merged, hp, mod_p[2], mod_p[4], mod_p[3], n2g, wo, wr_t, br_col,
                                         MIX_TM, t // MIX_TM)
        outs["kp"].append(k.reshape(bp, t, n_heads, dh))
        outs["vp"].append(v.reshape(bp, t, n_heads, dh))
        outs["srp"].append(srp.reshape(bp, n_groups, n_st))
        outs["sip"].append(sip.reshape(bp, n_groups, n_st))

        qs, ks, vs, xss, gas, gbs = _inproj(hs, mod_s[1], mod_s[0], n1g, w_in_bf, ns_tok, 1)
        pad_q = lambda a: jnp.pad(a.reshape(db, tq, aw), ((0, 0), (0, qp - tq), (0, 0)))
        q_pad, k_pad, v_pad = pad_q(qs), pad_q(ks), pad_q(vs)
        kmean = _cache_block_means(cache_k, l, page_table, n_past_blocks)
        sel = _select_blocks(q_pad, kmean.reshape(db, n_past_blocks, n_heads, dh), n_heads)
        sel_flat = sel[:, :, :MOBA_TOPK].reshape(db, n_heads * qp * MOBA_TOPK)
        att_s = _sample_attention(q_pad, k_pad, v_pad, sel_flat, page_table, cache_k, cache_v, l, n_heads, tq)
        att_s = att_s[:, :tq].reshape(ns_tok, aw).astype(BF16)
        gys, srs, sis = _s5(xss.reshape(1, ns_tok, sw), state_ssm_re[l].reshape(db, n_state),
                            state_ssm_im[l].reshape(db, n_state), abar_re_row, abar_im_row,
                            bre_bd, bim_bd, cre_bd, cim_bd, d_row, n_batch=db, rows_per_batch=tq)
        merged_s = _merge(att_s, gys.reshape(ns_tok, sw), gas, gbs, wbr, wv, wg, ns_tok)
        hs_mid, u2s, tis, gts = _outproj(merged_s, hs, mod_s[2], mod_s[4], mod_s[3], n2g, wo, wr_t, br_col,
                                         ns_tok, 1)
        outs["ks"].append(ks.reshape(db, tq, n_heads, dh))
        outs["vs"].append(vs.reshape(db, tq, n_heads, dh))
        outs["srs"].append(srs.reshape(db, n_groups, n_st))
        outs["sis"].append(sis.reshape(db, n_groups, n_st))

        n_pad = -(-n_tok // RANK_TM) * RANK_TM
        top_i = jnp.concatenate([tip, tis, jnp.full((MOE_TOPK, n_pad - n_tok), -1, I32)], axis=1)
        rank, counts = _expert_ranks(top_i, n_exp)
        padded = (counts + MOE_TM - 1) // MOE_TM * MOE_TM
        pad_end = jnp.cumsum(padded)
        pad_start = pad_end - padded
        dest = (pad_start[jnp.clip(top_i, 0, n_exp - 1)] + rank)[:, :n_tok]
        dest_t = dest.T
        n_blocks = -(-(n_tok * MOE_TOPK) // MOE_TM) + n_exp
        block_expert = jnp.minimum(
            jnp.searchsorted(pad_end, jnp.arange(n_blocks, dtype=I32) * MOE_TM, side="right"),
            n_exp - 1).astype(I32)
        n_used = (pad_end[-1:] // MOE_TM).astype(I32)

        rows = jnp.zeros((n_blocks * MOE_TM, d), F32)
        rows = _dispatch(dest_t[:np_tok].reshape(-1), u2p, rows, DISPATCH_TM)
        rows = _dispatch(dest_t[np_tok:].reshape(-1), u2s, rows, ns_tok)

        w_glu = w_gate_up[l][:, :, 0::2].astype(BF16)
        w_lin = w_gate_up[l][:, :, 1::2].astype(BF16)
        b_glu = b_gate_up[l][:, None, 0::2]
        b_lin = b_gate_up[l][:, None, 1::2]
        out_rows = _experts(rows, block_expert, n_used, w_glu, w_lin, w_down[l].astype(BF16),
                            b_glu, b_lin, b_down[l][:, None, :])

        gates_t = jnp.concatenate([gtp, gts], axis=1).T
        last = l == depth - 1
        fg = final_g[None] if last else None
        assert last, "depth > 1 needs the un-normalised residual stream"
        hp = _combine(dest_t[:np_tok].reshape(-1), gates_t[:np_tok], hp_mid, mod_p[5], fg, out_rows,
                      COMBINE_TM, t // COMBINE_TM)
        hs = _combine(dest_t[np_tok:].reshape(-1), gates_t[np_tok:], hs_mid, mod_s[5], fg, out_rows,
                      ns_tok, 1)

    st = lambda name: jnp.stack(outs[name])
    return (hp.reshape(bp, t, d), hs.reshape(db, tq, d),
            st("kp"), st("vp"), st("srp"), st("sip"), st("ks"), st("vs"), st("srs"), st("sis"))
```

```python
import functools
import math

import jax
import jax.numpy as jnp
from jax import lax
from jax.experimental import pallas as pl
from jax.experimental.pallas import tpu as pltpu

F32 = jnp.float32
BF16 = jnp.bfloat16
I32 = jnp.int32
HIGHEST = lax.Precision.HIGHEST

MOBA_BLOCK = 256
MOBA_TOPK = 3
SSM_GROUP = 16
MOE_TOPK = 4
SWIGLU_ALPHA = 1.702
SWIGLU_LIMIT = 7.0
RMS_EPS = 1e-5
N_MOD = 6

LANES = 128
SUBLANES = 8
VMEM_LIMIT_BYTES = 56 * 1024 * 1024

ADA_TN = 1024
PROJ_TN = 1024
PROJ_TM = 512
SCAN_TC = 64
SCAN_SLAB = 256
SCAN_COLS = 512
SCAN_UNROLL = 4
RANK_TM = 512
MOE_TM = 512
MOE_TF = 256
MOE_SUB = 256
MIX_TM = 256
DISPATCH_TM = 256
COMBINE_TM = 128
KMEAN_BLOCKS_PER_STEP = 4

NEG = -1e30


def _cparams(sem, vmem=VMEM_LIMIT_BYTES):
    return pltpu.CompilerParams(dimension_semantics=sem, vmem_limit_bytes=vmem)


def _dot(a, b):
    return jnp.dot(a, b, preferred_element_type=F32)


def _dot_nt(a, b, precision=None):
    return lax.dot_general(a, b, (((1,), (1,)), ((), ())), precision=precision,
                           preferred_element_type=F32)


def _sigmoid(x):
    return 1.0 / (1.0 + jnp.exp(-x))


def _rmsnorm(x, g):
    return (x * lax.rsqrt(jnp.mean(x * x, axis=-1, keepdims=True) + RMS_EPS)) * g


def _ada_kernel(c_ref, w_ref, b_ref, o_ref):
    c = c_ref[...]
    s = (c * _sigmoid(c)).astype(BF16)
    o_ref[...] = _dot(s, w_ref[...].astype(BF16)) + b_ref[...]


def _ada(c, w, b):
    r, d = c.shape
    m = w.shape[1]
    return pl.pallas_call(
        _ada_kernel,
        grid=(m // ADA_TN,),
        in_specs=[pl.BlockSpec((r, d), lambda j: (0, 0)),
                  pl.BlockSpec((d, ADA_TN), lambda j: (0, j)),
                  pl.BlockSpec((1, ADA_TN), lambda j: (0, j))],
        out_specs=pl.BlockSpec((r, ADA_TN), lambda j: (0, j)),
        out_shape=jax.ShapeDtypeStruct((r, m), F32),
        compiler_params=_cparams(("parallel",)),
        name="ada_mod",
    )(c, w, b)


def _inproj_kernel(x_ref, sc_ref, sh_ref, g_ref, w_ref,
                   q_ref, k_ref, v_ref, xs_ref, ga_ref, gb_ref, u_scr):
    j = pl.program_id(1)

    @pl.when(j == 0)
    def _():
        u = _rmsnorm(x_ref[...], g_ref[...]) * (1.0 + sc_ref[0]) + sh_ref[0]
        u_scr[...] = u.astype(BF16)

    r = _dot(u_scr[...], w_ref[...])
    for jj, ref in ((0, q_ref), (1, k_ref), (2, v_ref), (3, xs_ref)):
        @pl.when(j == jj)
        def _(ref=ref):
            ref[...] = r

    @pl.when((j == 4) | (j == 5))
    def _():
        ga_ref[...] = r.astype(BF16)

    @pl.when(j >= 6)
    def _():
        gb_ref[...] = r.astype(BF16)


def _inproj(x, sc, sh, g, w_bf, tm, tiles_per_group):
    n, d = x.shape
    tn = PROJ_TN
    assert w_bf.shape[1] == 8 * tn and d == 2 * tn, "projection layout: q|k|v|ssm each one tile, gates two"
    r = sc.shape[1]
    mod_spec = pl.BlockSpec((1, r, d), lambda i, j: (i // tiles_per_group, 0, 0))
    row = lambda i, j: (i, 0)
    outs = (
        jax.ShapeDtypeStruct((n, tn), F32),
        jax.ShapeDtypeStruct((n, tn), F32),
        jax.ShapeDtypeStruct((n, tn), F32),
        jax.ShapeDtypeStruct((n, tn), F32),
        jax.ShapeDtypeStruct((n, 2 * tn), BF16),
        jax.ShapeDtypeStruct((n, 2 * tn), BF16),
    )
    return pl.pallas_call(
        _inproj_kernel,
        grid=(n // tm, 8),
        in_specs=[pl.BlockSpec((tm, d), row), mod_spec, mod_spec,
                  pl.BlockSpec((1, d), lambda i, j: (0, 0)),
                  pl.BlockSpec((d, tn), lambda i, j: (0, j))],
        out_specs=(pl.BlockSpec((tm, tn), row), pl.BlockSpec((tm, tn), row),
                   pl.BlockSpec((tm, tn), row), pl.BlockSpec((tm, tn), row),
                   pl.BlockSpec((tm, tn), lambda i, j: (i, jnp.clip(j - 4, 0, 1))),
                   pl.BlockSpec((tm, tn), lambda i, j: (i, jnp.clip(j - 6, 0, 1)))),
        out_shape=outs,
        scratch_shapes=[pltpu.VMEM((tm, d), BF16)],
        compiler_params=_cparams(("parallel", "arbitrary")),
        name="in_proj",
    )(x, sc, sh, g, w_bf)


def _moba_prompt_kernel(q_ref, k_ref, v_ref, o_ref, kb_scr, vt_scr, km_scr, sel_scr, s_scr, *, nb, scale):
    qi = pl.program_id(2)
    blk = MOBA_BLOCK
    dh = q_ref.shape[-1]

    @pl.when(qi == 0)
    def _():
        k = k_ref[...]
        km_scr[...] = jnp.sum(k.reshape(nb, blk, dh), axis=1) * (1.0 / blk)
        for j in range(nb):
            kb_scr[j] = k[j * blk:(j + 1) * blk].astype(BF16)
            vt_scr[j] = v_ref[j * blk:(j + 1) * blk, :].T.astype(BF16)

    q = q_ref[...]
    gate_t = _dot_nt(km_scr[...], q, precision=HIGHEST)
    jidx = lax.broadcasted_iota(I32, gate_t.shape, 0)
    cnt = jnp.zeros(gate_t.shape, I32)
    for jp in range(nb):
        row = gate_t[jp:jp + 1, :]
        beats = (row > gate_t) | ((row == gate_t) & (jp < jidx))
        cnt = cnt + jnp.where(jp < qi, jnp.where(beats, 1, 0), 0)
    sel_scr[...] = jnp.where(((cnt < MOBA_TOPK) & (jidx < qi)) | (jidx == qi), 1.0, 0.0)

    qb = q.astype(BF16)
    diff = (lax.broadcasted_iota(I32, (blk, blk), 0) - lax.broadcasted_iota(I32, (blk, blk), 1))

    def scores(j, m):
        s = _dot_nt(kb_scr[j], qb) * scale
        ok = (sel_scr[pl.ds(j, 1), :] > 0.5) & (diff <= (qi - j) * blk)
        s = jnp.where(ok, s, NEG)
        s_scr[j] = s
        return jnp.maximum(m, jnp.max(s, axis=0, keepdims=True))

    m = lax.fori_loop(0, qi + 1, scores, jnp.full((1, blk), NEG, F32))

    def weigh(j, carry):
        l, acc = carry
        p = jnp.exp(s_scr[j] - m)
        return l + jnp.sum(p, axis=0, keepdims=True), acc + _dot(vt_scr[j], p.astype(BF16))

    l, acc = lax.fori_loop(0, qi + 1, weigh, (jnp.zeros((1, blk), F32), jnp.zeros((dh, blk), F32)))
    o_ref[...] = (acc / l).T.astype(o_ref.dtype)


def _moba_prompt(q, k, v, n_heads):
    b, t, w = q.shape
    dh = w // n_heads
    assert t % MOBA_BLOCK == 0 and dh == LANES
    nb = t // MOBA_BLOCK
    kern = functools.partial(_moba_prompt_kernel, nb=nb, scale=dh ** -0.5)
    kv_spec = pl.BlockSpec((None, t, dh), lambda bi, h, qi: (bi, 0, h))
    q_spec = pl.BlockSpec((None, MOBA_BLOCK, dh), lambda bi, h, qi: (bi, qi, h))
    return pl.pallas_call(
        kern,
        grid=(b, n_heads, nb),
        in_specs=[q_spec, kv_spec, kv_spec],
        out_specs=q_spec,
        out_shape=jax.ShapeDtypeStruct((b, t, w), BF16),
        scratch_shapes=[pltpu.VMEM((nb, MOBA_BLOCK, dh), BF16), pltpu.VMEM((nb, dh, MOBA_BLOCK), BF16),
                        pltpu.VMEM((nb, dh), F32), pltpu.VMEM((nb, MOBA_BLOCK), F32),
                        pltpu.VMEM((nb, MOBA_BLOCK, MOBA_BLOCK), F32)],
        compiler_params=_cparams(("parallel", "parallel", "arbitrary")),
        name="moba_prompt",
    )(q, k, v)


def _kmean_kernel(pt_ref, *refs, ppb, bps):
    del pt_ref
    o_ref = refs[ppb * bps]
    j = pl.program_id(1)
    for n in range(bps):
        s = jnp.sum(refs[n * ppb][...], axis=0)
        for p in range(1, ppb):
            s = s + jnp.sum(refs[n * ppb + p][...], axis=0)
        o_ref[pl.ds(j * bps + n, 1)] = (s * (1.0 / MOBA_BLOCK))[None]


def _cache_block_means(cache_k, layer, page_table, n_blocks):
    _, _, page, h, dh = cache_k.shape
    db = page_table.shape[0]
    ppb = MOBA_BLOCK // page
    bps = math.gcd(n_blocks, KMEAN_BLOCKS_PER_STEP)
    page_specs = [
        pl.BlockSpec((None, None, page, h, dh),
                     functools.partial(lambda b, j, pt, p: (layer, pt[b, j * ppb * bps + p], 0, 0, 0), p=p))
        for p in range(ppb * bps)]
    return pl.pallas_call(
        functools.partial(_kmean_kernel, ppb=ppb, bps=bps),
        grid_spec=pltpu.PrefetchScalarGridSpec(
            num_scalar_prefetch=1,
            grid=(db, n_blocks // bps),
            in_specs=page_specs,
            out_specs=pl.BlockSpec((None, n_blocks, h, dh), lambda b, j, pt: (b, 0, 0, 0))),
        out_shape=jax.ShapeDtypeStruct((db, n_blocks, h, dh), F32),
        compiler_params=_cparams(("parallel", "arbitrary")),
        name="cache_block_means",
    )(page_table, *([cache_k] * (ppb * bps)))


def _select_kernel(q_ref, km_ref, o_ref, *, n_heads, n_blocks):
    dh = km_ref.shape[-1]
    rows = []
    for h in range(n_heads):
        qh = q_ref[:, h * dh:(h + 1) * dh]
        rows.append(_dot_nt(qh, km_ref[:, h, :], precision=HIGHEST))
    g = jnp.concatenate(rows, axis=0)
    lane = lax.broadcasted_iota(I32, g.shape, 1)
    out_lane = lax.broadcasted_iota(I32, o_ref.shape, 1)
    out = jnp.zeros(o_ref.shape, I32)
    for t in range(MOBA_TOPK):
        m = jnp.max(g, axis=1, keepdims=True)
        idx = jnp.min(jnp.where(g == m, lane, n_blocks), axis=1, keepdims=True)
        out = jnp.where(out_lane == t, idx, out)
        g = jnp.where(lane == idx, -jnp.inf, g)
    o_ref[...] = out


def _select_blocks(q_pad, kmean, n_heads):
    db, qp, w = q_pad.shape
    n_blocks = kmean.shape[1]
    dh = w // n_heads
    return pl.pallas_call(
        functools.partial(_select_kernel, n_heads=n_heads, n_blocks=n_blocks),
        grid=(db,),
        in_specs=[pl.BlockSpec((None, qp, w), lambda b: (b, 0, 0)),
                  pl.BlockSpec((None, n_blocks, n_heads, dh), lambda b: (b, 0, 0, 0))],
        out_specs=pl.BlockSpec((None, n_heads * qp, LANES), lambda b: (b, 0, 0)),
        out_shape=jax.ShapeDtypeStruct((db, n_heads * qp, LANES), I32),
        compiler_params=_cparams(("parallel",)),
        name="moba_select",
    )(q_pad, kmean)


def _sample_attn_kernel(sel_ref, pt_ref, q_ref, kn_ref, vn_ref, ck_hbm, cv_hbm, o_ref,
                        kbuf, vbuf, sems, *, layer, n_heads, n_q, qp, page, scale):
    b = pl.program_id(0)
    ppb = MOBA_BLOCK // page
    dh = kbuf.shape[-1]
    per_head = n_q * MOBA_TOPK * ppb
    span = MOBA_TOPK * MOBA_BLOCK

    def copies(i):
        h = i // per_head
        rem = i % per_head
        qs = rem // ppb
        p = rem % ppb
        q = qs // MOBA_TOPK
        s = qs % MOBA_TOPK
        blk = sel_ref[b, (h * qp + q) * MOBA_TOPK + s]
        pg = pt_ref[b, blk * ppb + p]
        dst = pl.ds(qs * MOBA_BLOCK + p * page, page)
        return (pltpu.make_async_copy(ck_hbm.at[layer, pg, :, h, :], kbuf.at[h, dst, :], sems.at[0]),
                pltpu.make_async_copy(cv_hbm.at[layer, pg, :, h, :], vbuf.at[h, dst, :], sems.at[1]))

    def start(i, c):
        ck, cv = copies(i)
        ck.start()
        cv.start()
        return c

    def wait(i, c):
        ck, cv = copies(i)
        ck.wait()
        cv.wait()
        return c

    lax.fori_loop(0, n_heads * per_head, start, 0)
    lax.fori_loop(0, n_heads * per_head, wait, 0)

    n_keys = n_q * span
    row = lax.broadcasted_iota(I32, (qp, n_keys), 0)
    col = lax.broadcasted_iota(I32, (qp, n_keys), 1)
    own = (col >= row * span) & (col < (row + 1) * span)
    rown = lax.broadcasted_iota(I32, (qp, qp), 0)
    coln = lax.broadcasted_iota(I32, (qp, qp), 1)
    causal = (coln <= rown) & (coln < n_q)
    for h in range(n_heads):
        hs = slice(h * dh, (h + 1) * dh)
        qh = q_ref[:, hs].astype(BF16)
        s = jnp.where(own, _dot_nt(qh, kbuf[h].astype(BF16)) * scale, NEG)
        sn = jnp.where(causal, _dot_nt(qh, kn_ref[:, hs].astype(BF16)) * scale, NEG)
        m = jnp.maximum(jnp.max(s, axis=1, keepdims=True), jnp.max(sn, axis=1, keepdims=True))
        p = jnp.exp(s - m)
        pn = jnp.exp(sn - m)
        l = jnp.sum(p, axis=1, keepdims=True) + jnp.sum(pn, axis=1, keepdims=True)
        o = _dot(p.astype(BF16), vbuf[h].astype(BF16)) + _dot(pn.astype(BF16), vn_ref[:, hs].astype(BF16))
        o_ref[:, hs] = o / l


def _sample_attention(q_pad, k_pad, v_pad, sel_flat, page_table, cache_k, cache_v, layer, n_heads, n_q):
    db, qp, w = q_pad.shape
    dh = w // n_heads
    page = cache_k.shape[2]
    n_keys = n_q * MOBA_TOPK * MOBA_BLOCK
    kern = functools.partial(_sample_attn_kernel, layer=layer, n_heads=n_heads, n_q=n_q, qp=qp,
                             page=page, scale=dh ** -0.5)
    row_spec = pl.BlockSpec((None, qp, w), lambda b, sel, pt: (b, 0, 0))
    return pl.pallas_call(
        kern,
        grid_spec=pltpu.PrefetchScalarGridSpec(
            num_scalar_prefetch=2,
            grid=(db,),
            in_specs=[row_spec, row_spec, row_spec,
                      pl.BlockSpec(memory_space=pl.ANY), pl.BlockSpec(memory_space=pl.ANY)],
            out_specs=row_spec,
            scratch_shapes=[pltpu.VMEM((n_heads, n_keys, dh), F32),
                            pltpu.VMEM((n_heads, n_keys, dh), F32),
                            pltpu.SemaphoreType.DMA((2,))]),
        out_shape=jax.ShapeDtypeStruct((db, qp, w), F32),
        compiler_params=_cparams(("arbitrary",)),
        name="moba_sample",
    )(sel_flat, page_table, q_pad, k_pad, v_pad, cache_k, cache_v)


def _s5_param_kernel(are_ref, aim_ref, ldt_ref, bre_ref, bim_ref,
                     o_are, o_aim, o_bre, o_bim):
    a_re = are_ref[...]
    a_im = aim_ref[...]
    dt = jnp.exp(ldt_ref[...])
    mag = jnp.exp(a_re * dt)
    ab_re = mag * jnp.cos(a_im * dt)
    ab_im = mag * jnp.sin(a_im * dt)
    den = a_re * a_re + a_im * a_im
    nr = ab_re - 1.0
    ni = ab_im
    c_re = (nr * a_re + ni * a_im) / den
    c_im = (ni * a_re - nr * a_im) / den
    b_re = bre_ref[...]
    b_im = bim_ref[...]
    o_are[...] = ab_re
    o_aim[...] = ab_im
    o_bre[...] = c_re * b_re - c_im * b_im
    o_bim[...] = c_re * b_im + c_im * b_re


def _s5_params(a_re, a_im, log_dt, b_re, b_im):
    g, p = a_re.shape
    c = b_re.shape[-1]
    bt_re = jnp.swapaxes(b_re, 1, 2)
    bt_im = jnp.swapaxes(b_im, 1, 2)
    sds = jax.ShapeDtypeStruct
    ab_re, ab_im, bb_re, bb_im = pl.pallas_call(
        _s5_param_kernel,
        out_shape=(sds((g, 1, p), F32), sds((g, 1, p), F32), sds((g, c, p), F32), sds((g, c, p), F32)),
        name="s5_params",
    )(a_re.reshape(g, 1, p), a_im.reshape(g, 1, p), log_dt.reshape(g, 1, 1), bt_re, bt_im)
    return ab_re.reshape(g, p), ab_im.reshape(g, p), bb_re, bb_im


def _block_diag_slabs(m, groups_per_slab):
    g, r, c = m.shape
    n = g // groups_per_slab
    eye = jnp.eye(groups_per_slab, dtype=m.dtype)
    m4 = m.reshape(n, groups_per_slab, r, c)
    out = jnp.einsum("ngrc,gh->ngrhc", m4, eye)
    return out.reshape(n, groups_per_slab * r, groups_per_slab * c)


def _s5_kernel(xs_ref, s0re_ref, s0im_ref, are_ref, aim_ref, bre_ref, bim_ref, cre_ref, cim_ref, d_ref,
               gy_ref, fre_ref, fim_ref, u_t, y_t, sre, sim, st_re, st_im):
    c = pl.program_id(0)
    n_batch, rpb, w = xs_ref.shape
    n_slab, slab_in, slab_st = bre_ref.shape
    n_state = sre.shape[1]
    in_tiles = w // LANES
    tiles_per_slab = slab_in // LANES

    @pl.when(c == 0)
    def _():
        st_re[...] = s0re_ref[...]
        st_im[...] = s0im_ref[...]

    for b in range(n_batch):
        for ct in range(in_tiles):
            u_t[ct, pl.ds(b, rpb, stride=n_batch), :] = xs_ref[b, :, ct * LANES:(ct + 1) * LANES]

    for k in range(n_slab):
        u = jnp.concatenate([u_t[k * tiles_per_slab + i] for i in range(tiles_per_slab)], axis=1).astype(BF16)
        sre[:, k * slab_st:(k + 1) * slab_st] = _dot(u, bre_ref[k])
        sim[:, k * slab_st:(k + 1) * slab_st] = _dot(u, bim_ref[k])

    for bg in range(n_batch // SUBLANES):
        brows = slice(bg * SUBLANES, (bg + 1) * SUBLANES)
        for cs in range(n_state // SCAN_COLS):
            cols = slice(cs * SCAN_COLS, (cs + 1) * SCAN_COLS)
            a_re = jnp.broadcast_to(are_ref[:, cols], (SUBLANES, SCAN_COLS))
            a_im = jnp.broadcast_to(aim_ref[:, cols], (SUBLANES, SCAN_COLS))

            def step(t, carry, cols=cols, bg=bg, a_re=a_re, a_im=a_im):
                s_re, s_im = carry
                rows = pl.ds(pl.multiple_of(t * n_batch + bg * SUBLANES, SUBLANES), SUBLANES)
                n_re = a_re * s_re - a_im * s_im + sre[rows, cols]
                n_im = a_re * s_im + a_im * s_re + sim[rows, cols]
                sre[rows, cols] = n_re
                sim[rows, cols] = n_im
                return n_re, n_im

            f_re, f_im = lax.fori_loop(0, rpb, step, (st_re[brows, cols], st_im[brows, cols]),
                                       unroll=SCAN_UNROLL)
            st_re[brows, cols] = f_re
            st_im[brows, cols] = f_im

    fre_ref[...] = st_re[...]
    fim_ref[...] = st_im[...]

    for k in range(n_slab):
        scols = slice(k * slab_st, (k + 1) * slab_st)
        y = _dot(sre[:, scols].astype(BF16), cre_ref[k]) - _dot(sim[:, scols].astype(BF16), cim_ref[k])
        for i in range(tiles_per_slab):
            ct = k * tiles_per_slab + i
            yy = y[:, i * LANES:(i + 1) * LANES] + d_ref[:, ct * LANES:(ct + 1) * LANES] * u_t[ct]
            inner = math.sqrt(2.0 / math.pi) * (yy + 0.044715 * (yy * yy * yy))
            y_t[ct] = 0.5 * yy * (1.0 + jnp.tanh(inner))

    for b in range(n_batch):
        for ct in range(in_tiles):
            gy_ref[b, :, ct * LANES:(ct + 1) * LANES] = y_t[ct, pl.ds(b, rpb, stride=n_batch), :]


def _s5(xs3, s0_re, s0_im, abar_re, abar_im, bre_bd, bim_bd, cre_bd, cim_bd, d_row, rows_per_batch):
    n_batch, t, w = xs3.shape
    rpb = rows_per_batch
    n_state = abar_re.shape[1]
    rows = n_batch * rpb
    assert t % rpb == 0 and n_batch % SUBLANES == 0 and n_state % SCAN_COLS == 0 and w % LANES == 0
    full = lambda a: pl.BlockSpec(a.shape, lambda c: (0,) * a.ndim)
    st_spec = pl.BlockSpec((n_batch, n_state), lambda c: (0, 0))
    x_spec = pl.BlockSpec((n_batch, rpb, w), lambda c: (0, c, 0))
    sds = jax.ShapeDtypeStruct
    return pl.pallas_call(
        _s5_kernel,
        grid=(t // rpb,),
        in_specs=[x_spec, st_spec, st_spec, full(abar_re), full(abar_im), full(bre_bd), full(bim_bd),
                  full(cre_bd), full(cim_bd), full(d_row)],
        out_specs=(x_spec, st_spec, st_spec),
        out_shape=(sds((n_batch, t, w), F32), sds((n_batch, n_state), F32), sds((n_batch, n_state), F32)),
        scratch_shapes=[pltpu.VMEM((w // LANES, rows, LANES), F32), pltpu.VMEM((w // LANES, rows, LANES), F32),
                        pltpu.VMEM((rows, n_state), F32), pltpu.VMEM((rows, n_state), F32),
                        pltpu.VMEM((n_batch, n_state), F32), pltpu.VMEM((n_batch, n_state), F32)],
        compiler_params=_cparams(("arbitrary",)),
        name="s5_scan",
    )(xs3, s0_re, s0_im, abar_re, abar_im, bre_bd, bim_bd, cre_bd, cim_bd, d_row)


def _merge_kernel(att_ref, gy_ref, ga_ref, gb_ref, wbr_ref, wv_ref, wg_ref, o_ref):
    y_att = _dot(att_ref[...], wbr_ref[...])
    gy = gy_ref[...].astype(BF16)
    y_ssm = _dot(gy, wv_ref[...]) * _sigmoid(_dot(gy, wg_ref[...]))
    merged = _sigmoid(ga_ref[...].astype(F32)) * y_att + _sigmoid(gb_ref[...].astype(F32)) * y_ssm
    o_ref[...] = merged.astype(o_ref.dtype)


def _merge(att, gy, ga, gb, wbr, wv, wg, tm):
    n, aw = att.shape
    sw = gy.shape[1]
    d = ga.shape[1]
    row = lambda i: (i, 0)
    const = lambda i: (0, 0)
    return pl.pallas_call(
        _merge_kernel,
        grid=(n // tm,),
        in_specs=[pl.BlockSpec((tm, aw), row), pl.BlockSpec((tm, sw), row),
                  pl.BlockSpec((tm, d), row), pl.BlockSpec((tm, d), row),
                  pl.BlockSpec((aw, d), const), pl.BlockSpec((sw, d), const), pl.BlockSpec((sw, d), const)],
        out_specs=pl.BlockSpec((tm, d), row),
        out_shape=jax.ShapeDtypeStruct((n, d), BF16),
        compiler_params=_cparams(("parallel",)),
        name="branch_merge",
    )(att, gy, ga, gb, wbr, wv, wg)


def _outproj_kernel(m_ref, x_ref, g1_ref, sc_ref, sh_ref, ng_ref, wo_ref, wr_ref, br_ref,
                    h_ref, u_ref, ti_ref, gt_ref):
    h = x_ref[...] + g1_ref[0] * _dot(m_ref[...], wo_ref[...])
    h_ref[...] = h
    u = _rmsnorm(h, ng_ref[...]) * (1.0 + sc_ref[0]) + sh_ref[0]
    u_ref[...] = u
    logits = _dot_nt(wr_ref[...], u, precision=HIGHEST) + br_ref[...]
    n_exp = logits.shape[0]
    eidx = lax.broadcasted_iota(I32, logits.shape, 0)
    vals, idxs = [], []
    for _ in range(MOE_TOPK):
        m = jnp.max(logits, axis=0, keepdims=True)
        idx = jnp.min(jnp.where(logits == m, eidx, n_exp), axis=0, keepdims=True)
        vals.append(m)
        idxs.append(idx)
        logits = jnp.where(eidx == idx, -jnp.inf, logits)
    es = [jnp.exp(v - vals[0]) for v in vals]
    tot = es[0]
    for e in es[1:]:
        tot = tot + e
    ti_ref[...] = jnp.concatenate(idxs, axis=0)
    gt_ref[...] = jnp.concatenate([e / tot for e in es], axis=0)


def _outproj(merged, x, g1, sc, sh, ng, wo_bf, wr_t, br_col, tm, tiles_per_group):
    n, d = x.shape
    n_exp = wr_t.shape[0]
    r = sc.shape[1]
    mod_spec = pl.BlockSpec((1, r, d), lambda i: (i // tiles_per_group, 0, 0))
    row = lambda i: (i, 0)
    const = lambda i: (0, 0)
    sds = jax.ShapeDtypeStruct
    return pl.pallas_call(
        _outproj_kernel,
        grid=(n // tm,),
        in_specs=[pl.BlockSpec((tm, d), row), pl.BlockSpec((tm, d), row), mod_spec, mod_spec, mod_spec,
                  pl.BlockSpec((1, d), const), pl.BlockSpec((d, d), const),
                  pl.BlockSpec((n_exp, d), const), pl.BlockSpec((n_exp, 1), const)],
        out_specs=(pl.BlockSpec((tm, d), row), pl.BlockSpec((tm, d), row),
                   pl.BlockSpec((MOE_TOPK, tm), lambda i: (0, i)),
                   pl.BlockSpec((MOE_TOPK, tm), lambda i: (0, i))),
        out_shape=(sds((n, d), F32), sds((n, d), F32), sds((MOE_TOPK, n), I32), sds((MOE_TOPK, n), F32)),
        compiler_params=_cparams(("parallel",)),
        name="out_proj_router",
    )(merged, x, g1, sc, sh, ng, wo_bf, wr_t, br_col)


def _rank_kernel(ti_ref, rank_ref, cnt_ref, carry, *, n_exp):
    i = pl.program_id(0)
    tm = ti_ref.shape[1]

    @pl.when(i == 0)
    def _():
        carry[...] = jnp.zeros(carry.shape, F32)

    ti = ti_ref[...]
    eidx = lax.broadcasted_iota(I32, (n_exp, tm), 0)
    tri = jnp.where(lax.broadcasted_iota(I32, (tm, tm), 0) <= lax.broadcasted_iota(I32, (tm, tm), 1),
                    1.0, 0.0).astype(BF16)
    base = carry[...]
    ranks = []
    for k in range(MOE_TOPK):
        hit = eidx == ti[k:k + 1, :]
        onehot = jnp.where(hit, 1.0, 0.0)
        incl = _dot(onehot.astype(BF16), tri)
        ranks.append(jnp.sum(jnp.where(hit, base + incl - 1.0, 0.0), axis=0, keepdims=True))
        base = base + jnp.sum(onehot, axis=1, keepdims=True)
    carry[...] = base
    rank_ref[...] = jnp.concatenate(ranks, axis=0).astype(I32)
    cnt_ref[...] = jnp.broadcast_to(base, cnt_ref.shape)


def _expert_ranks(top_i, n_exp):
    k, n = top_i.shape
    rank, cnt = pl.pallas_call(
        functools.partial(_rank_kernel, n_exp=n_exp),
        grid=(n // RANK_TM,),
        in_specs=[pl.BlockSpec((k, RANK_TM), lambda i: (0, i))],
        out_specs=(pl.BlockSpec((k, RANK_TM), lambda i: (0, i)),
                   pl.BlockSpec((n_exp, LANES), lambda i: (0, 0))),
        out_shape=(jax.ShapeDtypeStruct((k, n), I32), jax.ShapeDtypeStruct((n_exp, LANES), F32)),
        scratch_shapes=[pltpu.VMEM((n_exp, 1), F32)],
        compiler_params=_cparams(("arbitrary",)),
        name="expert_ranks",
    )(top_i)
    return rank, cnt[:, 0].astype(I32)


def _dispatch_kernel(dest_ref, u_ref, rows_in, rows_out, sem):
    del rows_in
    i = pl.program_id(0)
    tm = u_ref.shape[0]
    base = i * tm * MOE_TOPK

    def copy(t, k, d):
        return pltpu.make_async_copy(u_ref.at[pl.ds(t, 1), :], rows_out.at[pl.ds(d, 1), :], sem)

    def start(t, c):
        for k in range(MOE_TOPK):
            copy(t, k, dest_ref[base + t * MOE_TOPK + k]).start()
        return c

    def wait(t, c):
        for k in range(MOE_TOPK):
            copy(0, k, 0).wait()
        return c

    lax.fori_loop(0, tm, start, 0)
    lax.fori_loop(0, tm, wait, 0)


def _dispatch(dest_flat, u, rows, tm):
    n, d = u.shape
    return pl.pallas_call(
        _dispatch_kernel,
        grid_spec=pltpu.PrefetchScalarGridSpec(
            num_scalar_prefetch=1,
            grid=(n // tm,),
            in_specs=[pl.BlockSpec((tm, d), lambda i, dest: (i, 0)),
                      pl.BlockSpec(memory_space=pl.ANY)],
            out_specs=pl.BlockSpec(memory_space=pl.ANY),
            scratch_shapes=[pltpu.SemaphoreType.DMA(())]),
        out_shape=jax.ShapeDtypeStruct(rows.shape, rows.dtype),
        input_output_aliases={2: 0},
        compiler_params=_cparams(("arbitrary",)),
        name="moe_dispatch",
    )(dest_flat, u, rows)


def _expert_kernel(be_ref, nu_ref, x_ref, wgu_ref, wd_ref, bgu_ref, bd_ref, pick_ref, o_ref, xb_scr):
    del be_ref
    i = pl.program_id(0)
    f = pl.program_id(1)

    @pl.when(i < nu_ref[0])
    def _():
        @pl.when(f == 0)
        def _():
            xb_scr[...] = x_ref[...].astype(BF16)
            o_ref[...] = jnp.broadcast_to(bd_ref[...], o_ref.shape)

        wgu = wgu_ref[...].astype(BF16)
        wd = wd_ref[...].astype(BF16)
        for r in range(o_ref.shape[0] // MOE_SUB):
            rs = slice(r * MOE_SUB, (r + 1) * MOE_SUB)
            h = _dot(xb_scr[rs, :], wgu) + bgu_ref[...]
            right = pltpu.roll(h, h.shape[1] - 1, 1)
            h_glu = jnp.minimum(h, SWIGLU_LIMIT)
            h_lin = jnp.clip(right, -SWIGLU_LIMIT, SWIGLU_LIMIT)
            act = h_glu * _sigmoid(SWIGLU_ALPHA * h_glu) * (h_lin + 1.0)
            act = _dot(act.astype(BF16), pick_ref[...])
            o_ref[rs, :] += _dot(act.astype(BF16), wd)

    @pl.when((i >= nu_ref[0]) & (f == 0))
    def _():
        o_ref[...] = jnp.zeros(o_ref.shape, o_ref.dtype)


def _experts(rows, block_expert, n_used, w_gate_up, w_down, b_gate_up, b_down):
    r, d = rows.shape
    n_exp, ff, _ = w_down.shape
    tm, tf = MOE_TM, MOE_TF
    nf = ff // tf
    nblk = r // tm
    pick = (jnp.arange(2 * tf, dtype=I32)[:, None] == 2 * jnp.arange(tf, dtype=I32)[None, :]).astype(BF16)

    def blk(i, nu):
        return jnp.minimum(i, nu[0] - 1)

    def fcol(i, f, nu):
        return jnp.where(i < nu[0], f, nf - 1)

    return pl.pallas_call(
        _expert_kernel,
        grid_spec=pltpu.PrefetchScalarGridSpec(
            num_scalar_prefetch=2,
            grid=(nblk, nf),
            in_specs=[
                pl.BlockSpec((tm, d), lambda i, f, be, nu: (blk(i, nu), 0)),
                pl.BlockSpec((None, d, 2 * tf), lambda i, f, be, nu: (be[blk(i, nu)], 0, fcol(i, f, nu))),
                pl.BlockSpec((None, tf, d), lambda i, f, be, nu: (be[blk(i, nu)], fcol(i, f, nu), 0)),
                pl.BlockSpec((None, 1, 2 * tf), lambda i, f, be, nu: (be[blk(i, nu)], 0, fcol(i, f, nu))),
                pl.BlockSpec((None, 1, d), lambda i, f, be, nu: (be[blk(i, nu)], 0, 0)),
                pl.BlockSpec((2 * tf, tf), lambda i, f, be, nu: (0, 0)),
            ],
            out_specs=pl.BlockSpec((tm, d), lambda i, f, be, nu: (i, 0)),
            scratch_shapes=[pltpu.VMEM((tm, d), BF16)]),
        out_shape=jax.ShapeDtypeStruct((r, d), F32),
        compiler_params=_cparams(("arbitrary", "arbitrary")),
        name="moe_experts",
    )(block_expert, n_used, rows, w_gate_up, w_down, b_gate_up, b_down, pick)


def _combine_kernel(dest_ref, gt_ref, h_ref, g2_ref, fg_ref, orow_hbm, y_ref, gath, sem):
    i = pl.program_id(0)
    tm = h_ref.shape[0]
    base = i * tm * MOE_TOPK

    def copy(t, k, d):
        return pltpu.make_async_copy(orow_hbm.at[pl.ds(d, 1), :], gath.at[k, pl.ds(t, 1), :], sem)

    def start(t, c):
        for k in range(MOE_TOPK):
            copy(t, k, dest_ref[base + t * MOE_TOPK + k]).start()
        return c

    def wait(t, c):
        for k in range(MOE_TOPK):
            copy(0, k, 0).wait()
        return c

    lax.fori_loop(0, tm, start, 0)
    lax.fori_loop(0, tm, wait, 0)

    gt = gt_ref[...]
    y = gt[:, 0:1] * gath[0]
    for k in range(1, MOE_TOPK):
        y = y + gt[:, k:k + 1] * gath[k]
    h = h_ref[...] + g2_ref[0] * y
    y_ref[...] = _rmsnorm(h, fg_ref[...])


def _combine(dest_flat, gates_t, h, g2, final_g, out_rows, tm, tiles_per_group):
    n, d = h.shape
    r = g2.shape[1]
    row = lambda i, dest: (i, 0)
    return pl.pallas_call(
        _combine_kernel,
        grid_spec=pltpu.PrefetchScalarGridSpec(
            num_scalar_prefetch=1,
            grid=(n // tm,),
            in_specs=[pl.BlockSpec((tm, MOE_TOPK), row), pl.BlockSpec((tm, d), row),
                      pl.BlockSpec((1, r, d), lambda i, dest: (i // tiles_per_group, 0, 0)),
                      pl.BlockSpec((1, d), lambda i, dest: (0, 0)),
                      pl.BlockSpec(memory_space=pl.ANY)],
            out_specs=pl.BlockSpec((tm, d), row),
            scratch_shapes=[pltpu.VMEM((MOE_TOPK, tm, d), F32), pltpu.SemaphoreType.DMA(())]),
        out_shape=jax.ShapeDtypeStruct((n, d), F32),
        compiler_params=_cparams(("arbitrary",)),
        name="moe_combine",
    )(dest_flat, gates_t, h, g2, final_g, out_rows)


def _mods(mod, rows_per_batch, per_row):
    b = mod.shape[0]
    parts = jnp.split(mod, N_MOD, axis=-1)
    if per_row:
        return [jnp.repeat(p, rows_per_batch, axis=0)[None] for p in parts]
    return [p.reshape(b, 1, -1) for p in parts]


def kernel(x_prompt, x_sample, cache_k, cache_v, state_ssm_re, state_ssm_im, page_table, c_prompt, c_sample,
           w_ada, b_ada, norm1_g, w_in, w_attn_br, ssm_a_re, ssm_a_im, ssm_log_dt, ssm_b_re, ssm_b_im,
           ssm_c_re, ssm_c_im, ssm_d, w_ssm_val, w_ssm_gate, w_out, norm2_g, w_router, b_router,
           w_gate_up, b_gate_up, w_down, b_down, final_g):
    depth = w_in.shape[0]
    bp, t, d = x_prompt.shape
    db, tq, _ = x_sample.shape
    _, n_pool, page, n_heads, dh = cache_k.shape
    aw = n_heads * dh
    n_groups, n_st = ssm_a_re.shape[1:]
    sw = n_groups * SSM_GROUP
    n_state = n_groups * n_st
    n_exp = w_router.shape[-1]
    past = page_table.shape[1] * page
    n_past_blocks = past // MOBA_BLOCK
    assert past % MOBA_BLOCK == 0 and MOBA_BLOCK % page == 0, "cached keys must fill whole MoBA blocks"
    assert n_past_blocks >= MOBA_TOPK and tq <= SUBLANES, "sample step: all selected blocks are past blocks"
    assert sw % SCAN_SLAB == 0 and t % SCAN_TC == 0 and t % PROJ_TM == 0

    np_tok = bp * t
    ns_tok = db * tq
    n_tok = np_tok + ns_tok
    qp = SUBLANES

    hp = x_prompt.reshape(np_tok, d)
    hs = x_sample.reshape(ns_tok, d)
    c_all = jnp.concatenate([c_prompt, c_sample], axis=0)

    outs = {k: [] for k in ("kp", "vp", "srp", "sip", "ks", "vs", "srs", "sis")}
    gps = SCAN_SLAB // SSM_GROUP
    for l in range(depth):
        mod = _ada(c_all, w_ada[l], b_ada[l][None])
        mod_p = _mods(mod[:bp], t, per_row=False)
        mod_s = _mods(mod[bp:], tq, per_row=True)

        w_in_bf = w_in[l].astype(BF16)
        wbr = w_attn_br[l].astype(BF16)
        wv = w_ssm_val[l].astype(BF16)
        wg = w_ssm_gate[l].astype(BF16)
        wo = w_out[l].astype(BF16)
        wr_t = w_router[l].T
        br_col = b_router[l][:, None]
        n1g = norm1_g[l][None]
        n2g = norm2_g[l][None]

        abar_re, abar_im, bbt_re, bbt_im = _s5_params(ssm_a_re[l], ssm_a_im[l], ssm_log_dt[l],
                                                      ssm_b_re[l], ssm_b_im[l])
        bre_bd = _block_diag_slabs(bbt_re, gps).astype(BF16)
        bim_bd = _block_diag_slabs(bbt_im, gps).astype(BF16)
        cre_bd = _block_diag_slabs(jnp.swapaxes(ssm_c_re[l], 1, 2), gps).astype(BF16)
        cim_bd = _block_diag_slabs(jnp.swapaxes(ssm_c_im[l], 1, 2), gps).astype(BF16)
        abar_re_row = abar_re.reshape(1, n_state)
        abar_im_row = abar_im.reshape(1, n_state)
        d_row = ssm_d[l].reshape(1, sw)

        tpg = t // PROJ_TM
        q, k, v, xs, ga, gb = _inproj(hp, mod_p[1], mod_p[0], n1g, w_in_bf, PROJ_TM, tpg)
        att = _moba_prompt(q.reshape(bp, t, aw), k.reshape(bp, t, aw), v.reshape(bp, t, aw), n_heads)
        zeros_state = jnp.zeros((bp, n_state), F32)
        gy, srp, sip = _s5(xs.reshape(bp, t, sw), zeros_state, zeros_state, abar_re_row, abar_im_row,
                           bre_bd, bim_bd, cre_bd, cim_bd, d_row, SCAN_TC)
        merged = _merge(att.reshape(np_tok, aw), gy.reshape(np_tok, sw), ga, gb, wbr, wv, wg, MIX_TM)
        hp_mid, u2p, tip, gtp = _outproj(merged, hp, mod_p[2], mod_p[4], mod_p[3], n2g, wo, wr_t, br_col,
                                         MIX_TM, t // MIX_TM)
        outs["kp"].append(k.reshape(bp, t, n_heads, dh))
        outs["vp"].append(v.reshape(bp, t, n_heads, dh))
        outs["srp"].append(srp.reshape(bp, n_groups, n_st))
        outs["sip"].append(sip.reshape(bp, n_groups, n_st))

        qs, ks, vs, xss, gas, gbs = _inproj(hs, mod_s[1], mod_s[0], n1g, w_in_bf, ns_tok, 1)
        pad_q = lambda a: jnp.pad(a.reshape(db, tq, aw), ((0, 0), (0, qp - tq), (0, 0)))
        q_pad, k_pad, v_pad = pad_q(qs), pad_q(ks), pad_q(vs)
        kmean = _cache_block_means(cache_k, l, page_table, n_past_blocks)
        sel = _select_blocks(q_pad, kmean.reshape(db, n_past_blocks, n_heads, dh), n_heads)
        sel_flat = sel[:, :, :MOBA_TOPK].reshape(db, n_heads * qp * MOBA_TOPK)
        att_s = _sample_attention(q_pad, k_pad, v_pad, sel_flat, page_table, cache_k, cache_v, l, n_heads, tq)
        att_s = att_s[:, :tq].reshape(ns_tok, aw).astype(BF16)
        gys, srs, sis = _s5(xss.reshape(db, tq, sw), state_ssm_re[l].reshape(db, n_state),
                            state_ssm_im[l].reshape(db, n_state), abar_re_row, abar_im_row,
                            bre_bd, bim_bd, cre_bd, cim_bd, d_row, tq)
        merged_s = _merge(att_s, gys.reshape(ns_tok, sw), gas, gbs, wbr, wv, wg, ns_tok)
        hs_mid, u2s, tis, gts = _outproj(merged_s, hs, mod_s[2], mod_s[4], mod_s[3], n2g, wo, wr_t, br_col,
                                         ns_tok, 1)
        outs["ks"].append(ks.reshape(db, tq, n_heads, dh))
        outs["vs"].append(vs.reshape(db, tq, n_heads, dh))
        outs["srs"].append(srs.reshape(db, n_groups, n_st))
        outs["sis"].append(sis.reshape(db, n_groups, n_st))

        n_pad = -(-n_tok // RANK_TM) * RANK_TM
        top_i = jnp.concatenate([tip, tis, jnp.full((MOE_TOPK, n_pad - n_tok), -1, I32)], axis=1)
        rank, counts = _expert_ranks(top_i, n_exp)
        padded = (counts + MOE_TM - 1) // MOE_TM * MOE_TM
        pad_end = jnp.cumsum(padded)
        pad_start = pad_end - padded
        is_exp = top_i[:, :n_tok, None] == jnp.arange(n_exp, dtype=I32)
        dest = jnp.sum(jnp.where(is_exp, pad_start, 0), axis=-1) + rank[:, :n_tok]
        dest_t = dest.T
        n_blocks = -(-(n_tok * MOE_TOPK) // MOE_TM) + n_exp
        block_row0 = jnp.arange(n_blocks, dtype=I32) * MOE_TM
        block_expert = jnp.minimum(jnp.sum(pad_end[None, :] <= block_row0[:, None], axis=1),
                                   n_exp - 1).astype(I32)
        n_used = (pad_end[-1:] // MOE_TM).astype(I32)

        rows = jnp.zeros((n_blocks * MOE_TM, d), F32)
        rows = _dispatch(dest_t[:np_tok].reshape(-1), u2p, rows, DISPATCH_TM)
        rows = _dispatch(dest_t[np_tok:].reshape(-1), u2s, rows, ns_tok)

        out_rows = _experts(rows, block_expert, n_used, w_gate_up[l], w_down[l],
                            b_gate_up[l][:, None, :], b_down[l][:, None, :])

        gates_t = jnp.concatenate([gtp, gts], axis=1).T
        last = l == depth - 1
        fg = final_g[None] if last else None
        assert last, "depth > 1 needs the un-normalised residual stream"
        hp = _combine(dest_t[:np_tok].reshape(-1), gates_t[:np_tok], hp_mid, mod_p[5], fg, out_rows,
                      COMBINE_TM, t // COMBINE_TM)
        hs = _combine(dest_t[np_tok:].reshape(-1), gates_t[np_tok:], hs_mid, mod_s[5], fg, out_rows,
                      ns_tok, 1)

    st = lambda name: jnp.stack(outs[name])
    return (hp.reshape(bp, t, d), hs.reshape(db, tq, d),
            st("kp"), st("vp"), st("srp"), st("sip"), st("ks"), st("vs"), st("srs"), st("sis"))
```

```python
import functools
import math

import jax
import jax.numpy as jnp
from jax import lax
from jax.experimental import pallas as pl
from jax.experimental.pallas import tpu as pltpu

F32 = jnp.float32
BF16 = jnp.bfloat16
I32 = jnp.int32
HIGHEST = lax.Precision.HIGHEST

MOBA_BLOCK = 256
MOBA_TOPK = 3
SSM_GROUP = 16
MOE_TOPK = 4
SWIGLU_ALPHA = 1.702
SWIGLU_LIMIT = 7.0
RMS_EPS = 1e-5
N_MOD = 6

LANES = 128
SUBLANES = 8
VMEM_LIMIT_BYTES = 56 * 1024 * 1024

ADA_TN = 1024
PROJ_TN = 1024
PROJ_TM = 512
SCAN_TC = 64
SCAN_SLAB = 256
SCAN_COLS = 512
SCAN_UNROLL = 4
RANK_TM = 512
MOE_SUB = 512
MOE_GROUP = 512
SLOT_BITS = 17
SLOT_MASK = (1 << SLOT_BITS) - 1
MIX_TM = 256
COMBINE_TM = 128
KMEAN_BLOCKS_PER_STEP = 4

NEG = -1e30


def _cparams(sem, vmem=VMEM_LIMIT_BYTES):
    return pltpu.CompilerParams(dimension_semantics=sem, vmem_limit_bytes=vmem)


def _dot(a, b):
    return jnp.dot(a, b, preferred_element_type=F32)


def _dot_nt(a, b, precision=None):
    return lax.dot_general(a, b, (((1,), (1,)), ((), ())), precision=precision,
                           preferred_element_type=F32)


def _sigmoid(x):
    return 1.0 / (1.0 + jnp.exp(-x))


def _rmsnorm(x, g):
    return (x * lax.rsqrt(jnp.mean(x * x, axis=-1, keepdims=True) + RMS_EPS)) * g


def _ada_kernel(c_ref, w_ref, b_ref, o_ref):
    c = c_ref[...]
    s = (c * _sigmoid(c)).astype(BF16)
    o_ref[...] = _dot(s, w_ref[...].astype(BF16)) + b_ref[...]


def _ada(c, w, b):
    r, d = c.shape
    m = w.shape[1]
    return pl.pallas_call(
        _ada_kernel,
        grid=(m // ADA_TN,),
        in_specs=[pl.BlockSpec((r, d), lambda j: (0, 0)),
                  pl.BlockSpec((d, ADA_TN), lambda j: (0, j)),
                  pl.BlockSpec((1, ADA_TN), lambda j: (0, j))],
        out_specs=pl.BlockSpec((r, ADA_TN), lambda j: (0, j)),
        out_shape=jax.ShapeDtypeStruct((r, m), F32),
        compiler_params=_cparams(("parallel",)),
        name="ada_mod",
    )(c, w, b)


def _inproj_kernel(x_ref, sc_ref, sh_ref, g_ref, w_ref,
                   q_ref, k_ref, v_ref, xs_ref, ga_ref, gb_ref, u_scr):
    j = pl.program_id(1)

    @pl.when(j == 0)
    def _():
        u = _rmsnorm(x_ref[...], g_ref[...]) * (1.0 + sc_ref[0]) + sh_ref[0]
        u_scr[...] = u.astype(BF16)

    r = _dot(u_scr[...], w_ref[...])
    for jj, ref in ((0, q_ref), (1, k_ref), (2, v_ref), (3, xs_ref)):
        @pl.when(j == jj)
        def _(ref=ref):
            ref[...] = r

    @pl.when((j == 4) | (j == 5))
    def _():
        ga_ref[...] = r.astype(BF16)

    @pl.when(j >= 6)
    def _():
        gb_ref[...] = r.astype(BF16)


def _inproj(x, sc, sh, g, w_bf, tm, tiles_per_group):
    n, d = x.shape
    tn = PROJ_TN
    assert w_bf.shape[1] == 8 * tn and d == 2 * tn, "projection layout: q|k|v|ssm each one tile, gates two"
    r = sc.shape[1]
    mod_spec = pl.BlockSpec((1, r, d), lambda i, j: (i // tiles_per_group, 0, 0))
    row = lambda i, j: (i, 0)
    outs = (
        jax.ShapeDtypeStruct((n, tn), F32),
        jax.ShapeDtypeStruct((n, tn), F32),
        jax.ShapeDtypeStruct((n, tn), F32),
        jax.ShapeDtypeStruct((n, tn), F32),
        jax.ShapeDtypeStruct((n, 2 * tn), BF16),
        jax.ShapeDtypeStruct((n, 2 * tn), BF16),
    )
    return pl.pallas_call(
        _inproj_kernel,
        grid=(n // tm, 8),
        in_specs=[pl.BlockSpec((tm, d), row), mod_spec, mod_spec,
                  pl.BlockSpec((1, d), lambda i, j: (0, 0)),
                  pl.BlockSpec((d, tn), lambda i, j: (0, j))],
        out_specs=(pl.BlockSpec((tm, tn), row), pl.BlockSpec((tm, tn), row),
                   pl.BlockSpec((tm, tn), row), pl.BlockSpec((tm, tn), row),
                   pl.BlockSpec((tm, tn), lambda i, j: (i, jnp.clip(j - 4, 0, 1))),
                   pl.BlockSpec((tm, tn), lambda i, j: (i, jnp.clip(j - 6, 0, 1)))),
        out_shape=outs,
        scratch_shapes=[pltpu.VMEM((tm, d), BF16)],
        compiler_params=_cparams(("parallel", "arbitrary")),
        name="in_proj",
    )(x, sc, sh, g, w_bf)


def _moba_prompt_kernel(q_ref, k_ref, v_ref, o_ref, kb_scr, vt_scr, km_scr, *, nb, scale):
    qi = pl.program_id(2)
    blk = MOBA_BLOCK
    dh = q_ref.shape[-1]

    @pl.when(qi == 0)
    def _():
        k = k_ref[...]
        km_scr[...] = jnp.sum(k.reshape(nb, blk, dh), axis=1) * (1.0 / blk)
        for j in range(nb):
            kb_scr[j] = k[j * blk:(j + 1) * blk].astype(BF16)
            vt_scr[j] = v_ref[j * blk:(j + 1) * blk, :].T.astype(BF16)

    q = q_ref[...]
    gate_t = _dot_nt(km_scr[...], q, precision=HIGHEST)
    jidx = lax.broadcasted_iota(I32, gate_t.shape, 0)
    cnt = jnp.zeros(gate_t.shape, I32)
    for jp in range(nb):
        row = gate_t[jp:jp + 1, :]
        beats = (row > gate_t) | ((row == gate_t) & (jp < jidx))
        cnt = cnt + jnp.where(jp < qi, jnp.where(beats, 1, 0), 0)
    sel = jnp.where((cnt < MOBA_TOPK) & (jidx < qi), 1.0, 0.0)

    qb = q.astype(BF16)
    causal = (lax.broadcasted_iota(I32, (blk, blk), 0) <= lax.broadcasted_iota(I32, (blk, blk), 1))

    def attend(own):
        blocks = []
        m = None
        for j in range(own + 1):
            s = _dot_nt(kb_scr[j], qb) * scale
            s = jnp.where(causal if j == own else sel[j:j + 1, :] > 0.5, s, NEG)
            blocks.append(s)
            mj = jnp.max(s, axis=0, keepdims=True)
            m = mj if m is None else jnp.maximum(m, mj)
        l = jnp.zeros((1, blk), F32)
        acc = jnp.zeros((dh, blk), F32)
        for j in range(own + 1):
            p = jnp.exp(blocks[j] - m)
            l = l + jnp.sum(p, axis=0, keepdims=True)
            acc = acc + _dot(vt_scr[j], p.astype(BF16))
        o_ref[...] = (acc / l).T.astype(o_ref.dtype)

    for own in range(nb):
        @pl.when(qi == own)
        def _(own=own):
            attend(own)


def _moba_prompt(q, k, v, n_heads):
    b, t, w = q.shape
    dh = w // n_heads
    assert t % MOBA_BLOCK == 0 and dh == LANES
    nb = t // MOBA_BLOCK
    kern = functools.partial(_moba_prompt_kernel, nb=nb, scale=dh ** -0.5)
    kv_spec = pl.BlockSpec((None, t, dh), lambda bi, h, qi: (bi, 0, h))
    q_spec = pl.BlockSpec((None, MOBA_BLOCK, dh), lambda bi, h, qi: (bi, qi, h))
    return pl.pallas_call(
        kern,
        grid=(b, n_heads, nb),
        in_specs=[q_spec, kv_spec, kv_spec],
        out_specs=q_spec,
        out_shape=jax.ShapeDtypeStruct((b, t, w), BF16),
        scratch_shapes=[pltpu.VMEM((nb, MOBA_BLOCK, dh), BF16), pltpu.VMEM((nb, dh, MOBA_BLOCK), BF16),
                        pltpu.VMEM((nb, dh), F32)],
        compiler_params=_cparams(("parallel", "parallel", "arbitrary")),
        name="moba_prompt",
    )(q, k, v)


def _kmean_kernel(pt_ref, *refs, ppb, bps):
    del pt_ref
    o_ref = refs[ppb * bps]
    j = pl.program_id(1)
    for n in range(bps):
        s = jnp.sum(refs[n * ppb][...], axis=0)
        for p in range(1, ppb):
            s = s + jnp.sum(refs[n * ppb + p][...], axis=0)
        o_ref[pl.ds(j * bps + n, 1)] = (s * (1.0 / MOBA_BLOCK))[None]


def _cache_block_means(cache_k, layer, page_table, n_blocks):
    _, _, page, h, dh = cache_k.shape
    db = page_table.shape[0]
    ppb = MOBA_BLOCK // page
    bps = math.gcd(n_blocks, KMEAN_BLOCKS_PER_STEP)
    page_specs = [
        pl.BlockSpec((None, None, page, h, dh),
                     functools.partial(lambda b, j, pt, p: (layer, pt[b, j * ppb * bps + p], 0, 0, 0), p=p))
        for p in range(ppb * bps)]
    return pl.pallas_call(
        functools.partial(_kmean_kernel, ppb=ppb, bps=bps),
        grid_spec=pltpu.PrefetchScalarGridSpec(
            num_scalar_prefetch=1,
            grid=(db, n_blocks // bps),
            in_specs=page_specs,
            out_specs=pl.BlockSpec((None, n_blocks, h, dh), lambda b, j, pt: (b, 0, 0, 0))),
        out_shape=jax.ShapeDtypeStruct((db, n_blocks, h, dh), F32),
        compiler_params=_cparams(("parallel", "arbitrary")),
        name="cache_block_means",
    )(page_table, *([cache_k] * (ppb * bps)))


def _select_kernel(q_ref, km_ref, o_ref, *, n_heads, n_blocks):
    dh = km_ref.shape[-1]
    rows = []
    for h in range(n_heads):
        qh = q_ref[:, h * dh:(h + 1) * dh]
        rows.append(_dot_nt(qh, km_ref[:, h, :], precision=HIGHEST))
    g = jnp.concatenate(rows, axis=0)
    lane = lax.broadcasted_iota(I32, g.shape, 1)
    out_lane = lax.broadcasted_iota(I32, o_ref.shape, 1)
    out = jnp.zeros(o_ref.shape, I32)
    for t in range(MOBA_TOPK):
        m = jnp.max(g, axis=1, keepdims=True)
        idx = jnp.min(jnp.where(g == m, lane, n_blocks), axis=1, keepdims=True)
        out = jnp.where(out_lane == t, idx, out)
        g = jnp.where(lane == idx, -jnp.inf, g)
    o_ref[...] = out


def _select_blocks(q_pad, kmean, n_heads):
    db, qp, w = q_pad.shape
    n_blocks = kmean.shape[1]
    dh = w // n_heads
    return pl.pallas_call(
        functools.partial(_select_kernel, n_heads=n_heads, n_blocks=n_blocks),
        grid=(db,),
        in_specs=[pl.BlockSpec((None, qp, w), lambda b: (b, 0, 0)),
                  pl.BlockSpec((None, n_blocks, n_heads, dh), lambda b: (b, 0, 0, 0))],
        out_specs=pl.BlockSpec((None, n_heads * qp, LANES), lambda b: (b, 0, 0)),
        out_shape=jax.ShapeDtypeStruct((db, n_heads * qp, LANES), I32),
        compiler_params=_cparams(("parallel",)),
        name="moba_select",
    )(q_pad, kmean)


def _sample_attn_kernel(sel_ref, pt_ref, q_ref, kn_ref, vn_ref, ck_hbm, cv_hbm, o_ref,
                        kbuf, vbuf, sems, *, layer, n_heads, n_q, qp, page, scale):
    b = pl.program_id(0)
    ppb = MOBA_BLOCK // page
    dh = kbuf.shape[-1]
    per_head = n_q * MOBA_TOPK * ppb
    span = MOBA_TOPK * MOBA_BLOCK

    def copies(i):
        h = i // per_head
        rem = i % per_head
        qs = rem // ppb
        p = rem % ppb
        q = qs // MOBA_TOPK
        s = qs % MOBA_TOPK
        blk = sel_ref[b, (h * qp + q) * MOBA_TOPK + s]
        pg = pt_ref[b, blk * ppb + p]
        dst = pl.ds(qs * MOBA_BLOCK + p * page, page)
        return (pltpu.make_async_copy(ck_hbm.at[layer, pg, :, h, :], kbuf.at[h, dst, :], sems.at[0]),
                pltpu.make_async_copy(cv_hbm.at[layer, pg, :, h, :], vbuf.at[h, dst, :], sems.at[1]))

    def start(i, c):
        ck, cv = copies(i)
        ck.start()
        cv.start()
        return c

    def wait(i, c):
        ck, cv = copies(i)
        ck.wait()
        cv.wait()
        return c

    lax.fori_loop(0, n_heads * per_head, start, 0)
    lax.fori_loop(0, n_heads * per_head, wait, 0)

    n_keys = n_q * span
    row = lax.broadcasted_iota(I32, (qp, n_keys), 0)
    col = lax.broadcasted_iota(I32, (qp, n_keys), 1)
    own = (col >= row * span) & (col < (row + 1) * span)
    rown = lax.broadcasted_iota(I32, (qp, qp), 0)
    coln = lax.broadcasted_iota(I32, (qp, qp), 1)
    causal = (coln <= rown) & (coln < n_q)
    for h in range(n_heads):
        hs = slice(h * dh, (h + 1) * dh)
        qh = q_ref[:, hs].astype(BF16)
        s = jnp.where(own, _dot_nt(qh, kbuf[h].astype(BF16)) * scale, NEG)
        sn = jnp.where(causal, _dot_nt(qh, kn_ref[:, hs].astype(BF16)) * scale, NEG)
        m = jnp.maximum(jnp.max(s, axis=1, keepdims=True), jnp.max(sn, axis=1, keepdims=True))
        p = jnp.exp(s - m)
        pn = jnp.exp(sn - m)
        l = jnp.sum(p, axis=1, keepdims=True) + jnp.sum(pn, axis=1, keepdims=True)
        o = _dot(p.astype(BF16), vbuf[h].astype(BF16)) + _dot(pn.astype(BF16), vn_ref[:, hs].astype(BF16))
        o_ref[:, hs] = o / l


def _sample_attention(q_pad, k_pad, v_pad, sel_flat, page_table, cache_k, cache_v, layer, n_heads, n_q):
    db, qp, w = q_pad.shape
    dh = w // n_heads
    page = cache_k.shape[2]
    n_keys = n_q * MOBA_TOPK * MOBA_BLOCK
    kern = functools.partial(_sample_attn_kernel, layer=layer, n_heads=n_heads, n_q=n_q, qp=qp,
                             page=page, scale=dh ** -0.5)
    row_spec = pl.BlockSpec((None, qp, w), lambda b, sel, pt: (b, 0, 0))
    return pl.pallas_call(
        kern,
        grid_spec=pltpu.PrefetchScalarGridSpec(
            num_scalar_prefetch=2,
            grid=(db,),
            in_specs=[row_spec, row_spec, row_spec,
                      pl.BlockSpec(memory_space=pl.ANY), pl.BlockSpec(memory_space=pl.ANY)],
            out_specs=row_spec,
            scratch_shapes=[pltpu.VMEM((n_heads, n_keys, dh), F32),
                            pltpu.VMEM((n_heads, n_keys, dh), F32),
                            pltpu.SemaphoreType.DMA((2,))]),
        out_shape=jax.ShapeDtypeStruct((db, qp, w), F32),
        compiler_params=_cparams(("arbitrary",)),
        name="moba_sample",
    )(sel_flat, page_table, q_pad, k_pad, v_pad, cache_k, cache_v)


def _s5_param_kernel(are_ref, aim_ref, ldt_ref, bre_ref, bim_ref,
                     o_are, o_aim, o_bre, o_bim):
    a_re = are_ref[...]
    a_im = aim_ref[...]
    dt = jnp.exp(ldt_ref[...])
    mag = jnp.exp(a_re * dt)
    ab_re = mag * jnp.cos(a_im * dt)
    ab_im = mag * jnp.sin(a_im * dt)
    den = a_re * a_re + a_im * a_im
    nr = ab_re - 1.0
    ni = ab_im
    c_re = (nr * a_re + ni * a_im) / den
    c_im = (ni * a_re - nr * a_im) / den
    b_re = bre_ref[...]
    b_im = bim_ref[...]
    o_are[...] = ab_re
    o_aim[...] = ab_im
    o_bre[...] = c_re * b_re - c_im * b_im
    o_bim[...] = c_re * b_im + c_im * b_re


def _s5_params(a_re, a_im, log_dt, b_re, b_im):
    g, p = a_re.shape
    c = b_re.shape[-1]
    bt_re = jnp.swapaxes(b_re, 1, 2)
    bt_im = jnp.swapaxes(b_im, 1, 2)
    sds = jax.ShapeDtypeStruct
    ab_re, ab_im, bb_re, bb_im = pl.pallas_call(
        _s5_param_kernel,
        out_shape=(sds((g, 1, p), F32), sds((g, 1, p), F32), sds((g, c, p), F32), sds((g, c, p), F32)),
        name="s5_params",
    )(a_re.reshape(g, 1, p), a_im.reshape(g, 1, p), log_dt.reshape(g, 1, 1), bt_re, bt_im)
    return ab_re.reshape(g, p), ab_im.reshape(g, p), bb_re, bb_im


def _block_diag_slabs(m, groups_per_slab):
    g, r, c = m.shape
    n = g // groups_per_slab
    eye = jnp.eye(groups_per_slab, dtype=m.dtype)
    m4 = m.reshape(n, groups_per_slab, r, c)
    out = jnp.einsum("ngrc,gh->ngrhc", m4, eye)
    return out.reshape(n, groups_per_slab * r, groups_per_slab * c)


def _s5_kernel(xs_ref, s0re_ref, s0im_ref, are_ref, aim_ref, bre_ref, bim_ref, cre_ref, cim_ref, d_ref,
               gy_ref, fre_ref, fim_ref, u_t, y_t, sre, sim, st_re, st_im):
    c = pl.program_id(0)
    n_batch, rpb, w = xs_ref.shape
    n_slab, slab_in, slab_st = bre_ref.shape
    n_state = sre.shape[1]
    in_tiles = w // LANES
    tiles_per_slab = slab_in // LANES

    @pl.when(c == 0)
    def _():
        st_re[...] = s0re_ref[...]
        st_im[...] = s0im_ref[...]

    for b in range(n_batch):
        for ct in range(in_tiles):
            u_t[ct, pl.ds(b, rpb, stride=n_batch), :] = xs_ref[b, :, ct * LANES:(ct + 1) * LANES]

    for k in range(n_slab):
        u = jnp.concatenate([u_t[k * tiles_per_slab + i] for i in range(tiles_per_slab)], axis=1).astype(BF16)
        sre[:, k * slab_st:(k + 1) * slab_st] = _dot(u, bre_ref[k])
        sim[:, k * slab_st:(k + 1) * slab_st] = _dot(u, bim_ref[k])

    for bg in range(n_batch // SUBLANES):
        brows = slice(bg * SUBLANES, (bg + 1) * SUBLANES)
        for cs in range(n_state // SCAN_COLS):
            cols = slice(cs * SCAN_COLS, (cs + 1) * SCAN_COLS)
            a_re = jnp.broadcast_to(are_ref[:, cols], (SUBLANES, SCAN_COLS))
            a_im = jnp.broadcast_to(aim_ref[:, cols], (SUBLANES, SCAN_COLS))

            def step(t, carry, cols=cols, bg=bg, a_re=a_re, a_im=a_im):
                s_re, s_im = carry
                rows = pl.ds(pl.multiple_of(t * n_batch + bg * SUBLANES, SUBLANES), SUBLANES)
                n_re = a_re * s_re - a_im * s_im + sre[rows, cols]
                n_im = a_re * s_im + a_im * s_re + sim[rows, cols]
                sre[rows, cols] = n_re
                sim[rows, cols] = n_im
                return n_re, n_im

            f_re, f_im = lax.fori_loop(0, rpb, step, (st_re[brows, cols], st_im[brows, cols]),
                                       unroll=SCAN_UNROLL)
            st_re[brows, cols] = f_re
            st_im[brows, cols] = f_im

    fre_ref[...] = st_re[...]
    fim_ref[...] = st_im[...]

    for k in range(n_slab):
        scols = slice(k * slab_st, (k + 1) * slab_st)
        y = _dot(sre[:, scols].astype(BF16), cre_ref[k]) - _dot(sim[:, scols].astype(BF16), cim_ref[k])
        for i in range(tiles_per_slab):
            ct = k * tiles_per_slab + i
            yy = y[:, i * LANES:(i + 1) * LANES] + d_ref[:, ct * LANES:(ct + 1) * LANES] * u_t[ct]
            inner = math.sqrt(2.0 / math.pi) * (yy + 0.044715 * (yy * yy * yy))
            y_t[ct] = 0.5 * yy * (1.0 + jnp.tanh(inner))

    for b in range(n_batch):
        for ct in range(in_tiles):
            gy_ref[b, :, ct * LANES:(ct + 1) * LANES] = y_t[ct, pl.ds(b, rpb, stride=n_batch), :]


def _s5(xs3, s0_re, s0_im, abar_re, abar_im, bre_bd, bim_bd, cre_bd, cim_bd, d_row, rows_per_batch):
    n_batch, t, w = xs3.shape
    rpb = rows_per_batch
    n_state = abar_re.shape[1]
    rows = n_batch * rpb
    assert t % rpb == 0 and n_batch % SUBLANES == 0 and n_state % SCAN_COLS == 0 and w % LANES == 0
    full = lambda a: pl.BlockSpec(a.shape, lambda c: (0,) * a.ndim)
    st_spec = pl.BlockSpec((n_batch, n_state), lambda c: (0, 0))
    x_spec = pl.BlockSpec((n_batch, rpb, w), lambda c: (0, c, 0))
    sds = jax.ShapeDtypeStruct
    return pl.pallas_call(
        _s5_kernel,
        grid=(t // rpb,),
        in_specs=[x_spec, st_spec, st_spec, full(abar_re), full(abar_im), full(bre_bd), full(bim_bd),
                  full(cre_bd), full(cim_bd), full(d_row)],
        out_specs=(x_spec, st_spec, st_spec),
        out_shape=(sds((n_batch, t, w), F32), sds((n_batch, n_state), F32), sds((n_batch, n_state), F32)),
        scratch_shapes=[pltpu.VMEM((w // LANES, rows, LANES), F32), pltpu.VMEM((w // LANES, rows, LANES), F32),
                        pltpu.VMEM((rows, n_state), F32), pltpu.VMEM((rows, n_state), F32),
                        pltpu.VMEM((n_batch, n_state), F32), pltpu.VMEM((n_batch, n_state), F32)],
        compiler_params=_cparams(("arbitrary",)),
        name="s5_scan",
    )(xs3, s0_re, s0_im, abar_re, abar_im, bre_bd, bim_bd, cre_bd, cim_bd, d_row)


def _merge_kernel(att_ref, gy_ref, ga_ref, gb_ref, wbr_ref, wv_ref, wg_ref, o_ref):
    y_att = _dot(att_ref[...], wbr_ref[...])
    gy = gy_ref[...].astype(BF16)
    y_ssm = _dot(gy, wv_ref[...]) * _sigmoid(_dot(gy, wg_ref[...]))
    merged = _sigmoid(ga_ref[...].astype(F32)) * y_att + _sigmoid(gb_ref[...].astype(F32)) * y_ssm
    o_ref[...] = merged.astype(o_ref.dtype)


def _merge(att, gy, ga, gb, wbr, wv, wg, tm):
    n, aw = att.shape
    sw = gy.shape[1]
    d = ga.shape[1]
    row = lambda i: (i, 0)
    const = lambda i: (0, 0)
    return pl.pallas_call(
        _merge_kernel,
        grid=(n // tm,),
        in_specs=[pl.BlockSpec((tm, aw), row), pl.BlockSpec((tm, sw), row),
                  pl.BlockSpec((tm, d), row), pl.BlockSpec((tm, d), row),
                  pl.BlockSpec((aw, d), const), pl.BlockSpec((sw, d), const), pl.BlockSpec((sw, d), const)],
        out_specs=pl.BlockSpec((tm, d), row),
        out_shape=jax.ShapeDtypeStruct((n, d), BF16),
        compiler_params=_cparams(("parallel",)),
        name="branch_merge",
    )(att, gy, ga, gb, wbr, wv, wg)


def _outproj_kernel(m_ref, x_ref, g1_ref, sc_ref, sh_ref, ng_ref, wo_ref, wr_ref, br_ref,
                    h_ref, u_ref, ti_ref, gt_ref):
    h = x_ref[...] + g1_ref[0] * _dot(m_ref[...], wo_ref[...])
    h_ref[...] = h
    u = _rmsnorm(h, ng_ref[...]) * (1.0 + sc_ref[0]) + sh_ref[0]
    u_ref[...] = u
    logits = _dot_nt(wr_ref[...], u, precision=HIGHEST) + br_ref[...]
    n_exp = logits.shape[0]
    eidx = lax.broadcasted_iota(I32, logits.shape, 0)
    vals, idxs = [], []
    for _ in range(MOE_TOPK):
        m = jnp.max(logits, axis=0, keepdims=True)
        idx = jnp.min(jnp.where(logits == m, eidx, n_exp), axis=0, keepdims=True)
        vals.append(m)
        idxs.append(idx)
        logits = jnp.where(eidx == idx, -jnp.inf, logits)
    es = [jnp.exp(v - vals[0]) for v in vals]
    tot = es[0]
    for e in es[1:]:
        tot = tot + e
    ti_ref[...] = jnp.concatenate(idxs, axis=0)
    gt_ref[...] = jnp.concatenate([e / tot for e in es], axis=0)


def _outproj(merged, x, g1, sc, sh, ng, wo_bf, wr_t, br_col, tm, tiles_per_group):
    n, d = x.shape
    n_exp = wr_t.shape[0]
    r = sc.shape[1]
    mod_spec = pl.BlockSpec((1, r, d), lambda i: (i // tiles_per_group, 0, 0))
    row = lambda i: (i, 0)
    const = lambda i: (0, 0)
    sds = jax.ShapeDtypeStruct
    return pl.pallas_call(
        _outproj_kernel,
        grid=(n // tm,),
        in_specs=[pl.BlockSpec((tm, d), row), pl.BlockSpec((tm, d), row), mod_spec, mod_spec, mod_spec,
                  pl.BlockSpec((1, d), const), pl.BlockSpec((d, d), const),
                  pl.BlockSpec((n_exp, d), const), pl.BlockSpec((n_exp, 1), const)],
        out_specs=(pl.BlockSpec((tm, d), row), pl.BlockSpec((tm, d), row),
                   pl.BlockSpec((MOE_TOPK, tm), lambda i: (0, i)),
                   pl.BlockSpec((MOE_TOPK, tm), lambda i: (0, i))),
        out_shape=(sds((n, d), F32), sds((n, d), F32), sds((MOE_TOPK, n), I32), sds((MOE_TOPK, n), F32)),
        compiler_params=_cparams(("parallel",)),
        name="out_proj_router",
    )(merged, x, g1, sc, sh, ng, wo_bf, wr_t, br_col)


def _rank_kernel(ti_ref, rank_ref, cnt_ref, carry, *, n_exp):
    i = pl.program_id(0)
    tm = ti_ref.shape[1]

    @pl.when(i == 0)
    def _():
        carry[...] = jnp.zeros(carry.shape, F32)

    ti = ti_ref[...]
    eidx = lax.broadcasted_iota(I32, (n_exp, tm), 0)
    tri = jnp.where(lax.broadcasted_iota(I32, (tm, tm), 0) <= lax.broadcasted_iota(I32, (tm, tm), 1),
                    1.0, 0.0).astype(BF16)
    base = carry[...]
    ranks = []
    for k in range(MOE_TOPK):
        hit = eidx == ti[k:k + 1, :]
        onehot = jnp.where(hit, 1.0, 0.0)
        incl = _dot(onehot.astype(BF16), tri)
        ranks.append(jnp.sum(jnp.where(hit, base + incl - 1.0, 0.0), axis=0, keepdims=True))
        base = base + jnp.sum(onehot, axis=1, keepdims=True)
    carry[...] = base
    rank_ref[...] = jnp.concatenate(ranks, axis=0).astype(I32)
    cnt_ref[...] = jnp.broadcast_to(base, cnt_ref.shape)


def _expert_ranks(top_i, n_exp):
    k, n = top_i.shape
    rank, cnt = pl.pallas_call(
        functools.partial(_rank_kernel, n_exp=n_exp),
        grid=(n // RANK_TM,),
        in_specs=[pl.BlockSpec((k, RANK_TM), lambda i: (0, i))],
        out_specs=(pl.BlockSpec((k, RANK_TM), lambda i: (0, i)),
                   pl.BlockSpec((n_exp, LANES), lambda i: (0, 0))),
        out_shape=(jax.ShapeDtypeStruct((k, n), I32), jax.ShapeDtypeStruct((n_exp, LANES), F32)),
        scratch_shapes=[pltpu.VMEM((n_exp, 1), F32)],
        compiler_params=_cparams(("arbitrary",)),
        name="expert_ranks",
    )(top_i)
    return rank, cnt[:, 0].astype(I32)


def _wprep_kernel(w_ref, perm_ref, o_ref):
    o_ref[...] = _dot(w_ref[...].astype(BF16), perm_ref[...]).astype(BF16)


def _prep_gate_up(w_gu):
    n_exp, d, ff2 = w_gu.shape
    g = MOE_GROUP
    col = jnp.arange(g, dtype=I32)
    target = jnp.where(col % 2 == 0, col // 2, g // 2 + col // 2)
    perm = (target[:, None] == col[None, :]).astype(BF16)
    return pl.pallas_call(
        _wprep_kernel,
        grid=(n_exp, ff2 // g),
        in_specs=[pl.BlockSpec((None, d, g), lambda e, j: (e, 0, j)),
                  pl.BlockSpec((g, g), lambda e, j: (0, 0))],
        out_specs=pl.BlockSpec((None, d, g), lambda e, j: (e, 0, j)),
        out_shape=jax.ShapeDtypeStruct((n_exp, d, ff2), BF16),
        compiler_params=_cparams(("parallel", "parallel")),
        name="moe_weight_prep",
    )(w_gu, perm)


def _group_bias(b_gu):
    n_exp, ff2 = b_gu.shape
    b = b_gu.reshape(n_exp, ff2 // MOE_GROUP, MOE_GROUP // 2, 2)
    return jnp.swapaxes(b, 2, 3).reshape(n_exp, 1, ff2)


def _expert_kernel(ss_ref, ns_ref, pk_ref, bgu_ref, bd_ref, u_hbm, wgu_hbm, wd_hbm, slots_hbm,
                   wgu_v, wd_v, x_st, xb_scr, act_scr, o_st, sem_w, sem_g, sem_s, *, n_exp):
    e = pl.program_id(0)
    sub0 = ss_ref[e]
    nsub = ns_ref[e]
    sub = MOE_SUB
    n_grp = wgu_v.shape[1] // MOE_GROUP
    half = MOE_GROUP // 2
    chunk = sub // n_grp

    def gather(row, slot, r):
        tok = lax.shift_right_logical(pk_ref[row], jnp.int32(SLOT_BITS))
        return pltpu.make_async_copy(u_hbm.at[pl.ds(tok, 1), :], x_st.at[slot, pl.ds(r, 1), :], sem_g.at[slot])

    def scatter(row, slot, r):
        dst = pk_ref[row] & SLOT_MASK
        return pltpu.make_async_copy(o_st.at[slot, pl.ds(r, 1), :], slots_hbm.at[pl.ds(dst, 1), :],
                                     sem_s.at[slot])

    def wait_rows(make, slot):
        def body(r, c):
            make(0, slot, 0).wait()
            return c
        lax.fori_loop(0, sub, body, 0, unroll=8)

    def start_rows(make, base, slot):
        def body(r, c):
            make(base + r, slot, r).start()
            return c
        lax.fori_loop(0, sub, body, 0, unroll=8)

    def compute(slot, issue):
        xb_scr[...] = x_st[slot].astype(BF16)
        for g in range(n_grp):
            cols = slice(g * MOE_GROUP, (g + 1) * MOE_GROUP)
            h = _dot(xb_scr[...], wgu_v[:, cols]) + bgu_ref[:, cols]
            h_glu = jnp.minimum(h[:, :half], SWIGLU_LIMIT)
            h_lin = jnp.clip(h[:, half:], -SWIGLU_LIMIT, SWIGLU_LIMIT)
            act = h_glu * _sigmoid(SWIGLU_ALPHA * h_glu) * (h_lin + 1.0)
            act_scr[:, g * half:(g + 1) * half] = act.astype(BF16)
            issue(range(g * chunk, (g + 1) * chunk))
        o_st[slot] = _dot(act_scr[...], wd_v[...]) + bd_ref[...]

    w_gu_copy = pltpu.make_async_copy(wgu_hbm.at[e], wgu_v, sem_w.at[0])
    w_d_copy = pltpu.make_async_copy(wd_hbm.at[e], wd_v, sem_w.at[1])
    w_gu_copy.start()
    w_d_copy.start()

    @pl.when(nsub > 0)
    def _():
        start_rows(gather, sub0 * sub, 0)

    w_gu_copy.wait()
    w_d_copy.wait()

    @pl.when(nsub > 0)
    def _():
        wait_rows(gather, 0)

        def issue_first(rows):
            for r in rows:
                gather((sub0 + 1) * sub + r, 1, r).start()

        compute(0, issue_first)

        def body(s, c):
            slot = s % 2
            other = 1 - slot
            wait_rows(gather, slot)

            def issue(rows):
                for r in rows:
                    gather((sub0 + s + 1) * sub + r, other, r).start()
                    scatter((sub0 + s - 1) * sub + r, other, r).start()

            compute(slot, issue)
            wait_rows(scatter, other)
            return c

        lax.fori_loop(1, nsub, body, 0)
        last = nsub - 1
        lslot = last % 2
        wait_rows(gather, 1 - lslot)
        start_rows(scatter, (sub0 + last) * sub, lslot)
        wait_rows(scatter, lslot)

    @pl.when(e == n_exp - 1)
    def _():
        o_st[0] = jnp.zeros(o_st.shape[1:], o_st.dtype)

        def tail(i, c):
            start_rows(scatter, (ss_ref[n_exp] + i) * sub, 0)
            wait_rows(scatter, 0)
            return c

        lax.fori_loop(0, ns_ref[n_exp], tail, 0)


def _experts(u_all, sub_start, n_sub, packed, w_gu_bf, w_d_bf, b_gu, b_d, n_rows):
    n, d = u_all.shape
    n_exp, _, ff2 = w_gu_bf.shape
    ff = ff2 // 2
    sub = MOE_SUB
    assert ff2 % MOE_GROUP == 0 and sub % (ff2 // MOE_GROUP) == 0
    return pl.pallas_call(
        functools.partial(_expert_kernel, n_exp=n_exp),
        grid_spec=pltpu.PrefetchScalarGridSpec(
            num_scalar_prefetch=3,
            grid=(n_exp,),
            in_specs=[pl.BlockSpec((None, 1, ff2), lambda e, ss, ns, pk: (e, 0, 0)),
                      pl.BlockSpec((None, 1, d), lambda e, ss, ns, pk: (e, 0, 0)),
                      pl.BlockSpec(memory_space=pl.ANY), pl.BlockSpec(memory_space=pl.ANY),
                      pl.BlockSpec(memory_space=pl.ANY)],
            out_specs=pl.BlockSpec(memory_space=pl.ANY),
            scratch_shapes=[pltpu.VMEM((d, ff2), BF16), pltpu.VMEM((ff, d), BF16),
                            pltpu.VMEM((2, sub, d), F32), pltpu.VMEM((sub, d), BF16),
                            pltpu.VMEM((sub, ff), BF16), pltpu.VMEM((2, sub, d), F32),
                            pltpu.SemaphoreType.DMA((2,)), pltpu.SemaphoreType.DMA((2,)),
                            pltpu.SemaphoreType.DMA((2,))]),
        out_shape=jax.ShapeDtypeStruct((n_rows, d), F32),
        compiler_params=_cparams(("arbitrary",)),
        name="moe_experts",
    )(sub_start, n_sub, packed, b_gu, b_d, u_all, w_gu_bf, w_d_bf)


def _combine_kernel(*refs):
    slot_refs = refs[:MOE_TOPK]
    gt_ref, h_ref, g2_ref, fg_ref, y_ref = refs[MOE_TOPK:]
    gt = gt_ref[...]
    y = gt[:, 0:1] * slot_refs[0][...]
    for k in range(1, MOE_TOPK):
        y = y + gt[:, k:k + 1] * slot_refs[k][...]
    h = h_ref[...] + g2_ref[0] * y
    y_ref[...] = _rmsnorm(h, fg_ref[...])


def _combine(slots, n_all, tok0, gates_t, h, g2, final_g, tm, tiles_per_group):
    n, d = h.shape
    r = g2.shape[1]
    assert n_all % tm == 0 and tok0 % tm == 0
    row = lambda i: (i, 0)
    slot_specs = [pl.BlockSpec((tm, d), functools.partial(lambda i, k: ((k * n_all + tok0) // tm + i, 0), k=k))
                  for k in range(MOE_TOPK)]
    return pl.pallas_call(
        _combine_kernel,
        grid=(n // tm,),
        in_specs=slot_specs + [pl.BlockSpec((tm, MOE_TOPK), row), pl.BlockSpec((tm, d), row),
                               pl.BlockSpec((1, r, d), lambda i: (i // tiles_per_group, 0, 0)),
                               pl.BlockSpec((1, d), lambda i: (0, 0))],
        out_specs=pl.BlockSpec((tm, d), row),
        out_shape=jax.ShapeDtypeStruct((n, d), F32),
        compiler_params=_cparams(("parallel",)),
        name="moe_combine",
    )(*([slots] * MOE_TOPK), gates_t, h, g2, final_g)


def _mods(mod, rows_per_batch, per_row):
    b = mod.shape[0]
    parts = jnp.split(mod, N_MOD, axis=-1)
    if per_row:
        return [jnp.repeat(p, rows_per_batch, axis=0)[None] for p in parts]
    return [p.reshape(b, 1, -1) for p in parts]


def kernel(x_prompt, x_sample, cache_k, cache_v, state_ssm_re, state_ssm_im, page_table, c_prompt, c_sample,
           w_ada, b_ada, norm1_g, w_in, w_attn_br, ssm_a_re, ssm_a_im, ssm_log_dt, ssm_b_re, ssm_b_im,
           ssm_c_re, ssm_c_im, ssm_d, w_ssm_val, w_ssm_gate, w_out, norm2_g, w_router, b_router,
           w_gate_up, b_gate_up, w_down, b_down, final_g):
    depth = w_in.shape[0]
    bp, t, d = x_prompt.shape
    db, tq, _ = x_sample.shape
    _, n_pool, page, n_heads, dh = cache_k.shape
    aw = n_heads * dh
    n_groups, n_st = ssm_a_re.shape[1:]
    sw = n_groups * SSM_GROUP
    n_state = n_groups * n_st
    n_exp = w_router.shape[-1]
    past = page_table.shape[1] * page
    n_past_blocks = past // MOBA_BLOCK
    assert past % MOBA_BLOCK == 0 and MOBA_BLOCK % page == 0, "cached keys must fill whole MoBA blocks"
    assert n_past_blocks >= MOBA_TOPK and tq <= SUBLANES, "sample step: all selected blocks are past blocks"
    assert sw % SCAN_SLAB == 0 and t % SCAN_TC == 0 and t % PROJ_TM == 0

    np_tok = bp * t
    ns_tok = db * tq
    n_tok = np_tok + ns_tok
    qp = SUBLANES

    hp = x_prompt.reshape(np_tok, d)
    hs = x_sample.reshape(ns_tok, d)
    c_all = jnp.concatenate([c_prompt, c_sample], axis=0)

    outs = {k: [] for k in ("kp", "vp", "srp", "sip", "ks", "vs", "srs", "sis")}
    gps = SCAN_SLAB // SSM_GROUP
    for l in range(depth):
        mod = _ada(c_all, w_ada[l], b_ada[l][None])
        mod_p = _mods(mod[:bp], t, per_row=False)
        mod_s = _mods(mod[bp:], tq, per_row=True)

        w_in_bf = w_in[l].astype(BF16)
        wbr = w_attn_br[l].astype(BF16)
        wv = w_ssm_val[l].astype(BF16)
        wg = w_ssm_gate[l].astype(BF16)
        wo = w_out[l].astype(BF16)
        wr_t = w_router[l].T
        br_col = b_router[l][:, None]
        n1g = norm1_g[l][None]
        n2g = norm2_g[l][None]

        abar_re, abar_im, bbt_re, bbt_im = _s5_params(ssm_a_re[l], ssm_a_im[l], ssm_log_dt[l],
                                                      ssm_b_re[l], ssm_b_im[l])
        bre_bd = _block_diag_slabs(bbt_re, gps).astype(BF16)
        bim_bd = _block_diag_slabs(bbt_im, gps).astype(BF16)
        cre_bd = _block_diag_slabs(jnp.swapaxes(ssm_c_re[l], 1, 2), gps).astype(BF16)
        cim_bd = _block_diag_slabs(jnp.swapaxes(ssm_c_im[l], 1, 2), gps).astype(BF16)
        abar_re_row = abar_re.reshape(1, n_state)
        abar_im_row = abar_im.reshape(1, n_state)
        d_row = ssm_d[l].reshape(1, sw)

        tpg = t // PROJ_TM
        q, k, v, xs, ga, gb = _inproj(hp, mod_p[1], mod_p[0], n1g, w_in_bf, PROJ_TM, tpg)
        att = _moba_prompt(q.reshape(bp, t, aw), k.reshape(bp, t, aw), v.reshape(bp, t, aw), n_heads)
        zeros_state = jnp.zeros((bp, n_state), F32)
        gy, srp, sip = _s5(xs.reshape(bp, t, sw), zeros_state, zeros_state, abar_re_row, abar_im_row,
                           bre_bd, bim_bd, cre_bd, cim_bd, d_row, SCAN_TC)
        merged = _merge(att.reshape(np_tok, aw), gy.reshape(np_tok, sw), ga, gb, wbr, wv, wg, MIX_TM)
        hp_mid, u2p, tip, gtp = _outproj(merged, hp, mod_p[2], mod_p[4], mod_p[3], n2g, wo, wr_t, br_col,
                                         MIX_TM, t // MIX_TM)
        outs["kp"].append(k.reshape(bp, t, n_heads, dh))
        outs["vp"].append(v.reshape(bp, t, n_heads, dh))
        outs["srp"].append(srp.reshape(bp, n_groups, n_st))
        outs["sip"].append(sip.reshape(bp, n_groups, n_st))

        qs, ks, vs, xss, gas, gbs = _inproj(hs, mod_s[1], mod_s[0], n1g, w_in_bf, ns_tok, 1)
        pad_q = lambda a: jnp.pad(a.reshape(db, tq, aw), ((0, 0), (0, qp - tq), (0, 0)))
        q_pad, k_pad, v_pad = pad_q(qs), pad_q(ks), pad_q(vs)
        kmean = _cache_block_means(cache_k, l, page_table, n_past_blocks)
        sel = _select_blocks(q_pad, kmean.reshape(db, n_past_blocks, n_heads, dh), n_heads)
        sel_flat = sel[:, :, :MOBA_TOPK].reshape(db, n_heads * qp * MOBA_TOPK)
        att_s = _sample_attention(q_pad, k_pad, v_pad, sel_flat, page_table, cache_k, cache_v, l, n_heads, tq)
        att_s = att_s[:, :tq].reshape(ns_tok, aw).astype(BF16)
        gys, srs, sis = _s5(xss.reshape(db, tq, sw), state_ssm_re[l].reshape(db, n_state),
                            state_ssm_im[l].reshape(db, n_state), abar_re_row, abar_im_row,
                            bre_bd, bim_bd, cre_bd, cim_bd, d_row, tq)
        merged_s = _merge(att_s, gys.reshape(ns_tok, sw), gas, gbs, wbr, wv, wg, ns_tok)
        hs_mid, u2s, tis, gts = _outproj(merged_s, hs, mod_s[2], mod_s[4], mod_s[3], n2g, wo, wr_t, br_col,
                                         ns_tok, 1)
        outs["ks"].append(ks.reshape(db, tq, n_heads, dh))
        outs["vs"].append(vs.reshape(db, tq, n_heads, dh))
        outs["srs"].append(srs.reshape(db, n_groups, n_st))
        outs["sis"].append(sis.reshape(db, n_groups, n_st))

        n_pad = -(-n_tok // RANK_TM) * RANK_TM
        top_i = jnp.concatenate([tip, tis, jnp.full((MOE_TOPK, n_pad - n_tok), -1, I32)], axis=1)
        rank, counts = _expert_ranks(top_i, n_exp)
        padded = (counts + MOE_SUB - 1) // MOE_SUB * MOE_SUB
        pad_end = jnp.cumsum(padded)
        pad_start = pad_end - padded
        is_exp = top_i[:, :n_tok, None] == jnp.arange(n_exp, dtype=I32)
        dest = jnp.sum(jnp.where(is_exp, pad_start, 0), axis=-1) + rank[:, :n_tok]
        n_sub_max = -(-(n_tok * MOE_TOPK) // MOE_SUB) + n_exp
        n_rows = n_sub_max * MOE_SUB
        assert n_tok < (1 << (32 - SLOT_BITS)) and n_rows <= (1 << SLOT_BITS)
        tok_ids = jnp.broadcast_to(jnp.arange(n_tok, dtype=jnp.uint32), (MOE_TOPK, n_tok))
        real = (tok_ids << SLOT_BITS) | (tok_ids + jnp.arange(MOE_TOPK, dtype=jnp.uint32)[:, None] * n_tok)
        empty = jnp.uint32(0xFFFFFFFF)
        desc = jnp.full((n_rows,), empty, jnp.uint32).at[dest.reshape(-1)].set(real.reshape(-1))
        is_empty = desc == empty
        spare = (n_tok * MOE_TOPK - 1 + jnp.cumsum(is_empty.astype(I32))).astype(jnp.uint32)
        desc = jnp.where(is_empty, spare, desc)
        packed = lax.bitcast_convert_type(jnp.concatenate([desc, jnp.zeros((MOE_SUB,), jnp.uint32)]), I32)
        used_sub = pad_end[-1:] // MOE_SUB
        sub_start = jnp.concatenate([pad_start // MOE_SUB, used_sub]).astype(I32)
        n_sub = jnp.concatenate([padded // MOE_SUB, n_sub_max - used_sub]).astype(I32)

        u_all = jnp.concatenate([u2p, u2s], axis=0)
        slots = _experts(u_all, sub_start, n_sub, packed, _prep_gate_up(w_gate_up[l]), w_down[l].astype(BF16),
                         _group_bias(b_gate_up[l]), b_down[l][:, None, :], n_rows)

        gates_t = jnp.concatenate([gtp, gts], axis=1).T
        last = l == depth - 1
        fg = final_g[None] if last else None
        assert last, "depth > 1 needs the un-normalised residual stream"
        hp = _combine(slots, n_tok, 0, gates_t[:np_tok], hp_mid, mod_p[5], fg, COMBINE_TM, t // COMBINE_TM)
        hs = _combine(slots, n_tok, np_tok, gates_t[np_tok:], hs_mid, mod_s[5], fg, ns_tok, 1)

    st = lambda name: jnp.stack(outs[name])
    return (hp.reshape(bp, t, d), hs.reshape(db, tq, d),
            st("kp"), st("vp"), st("srp"), st("sip"), st("ks"), st("vs"), st("srs"), st("sis"))
```

```python
import functools
import math

import jax
import jax.numpy as jnp
from jax import lax
from jax.experimental import pallas as pl
from jax.experimental.pallas import tpu as pltpu

F32 = jnp.float32
BF16 = jnp.bfloat16
I32 = jnp.int32
HIGHEST = lax.Precision.HIGHEST

MOBA_BLOCK = 256
MOBA_TOPK = 3
SSM_GROUP = 16
MOE_TOPK = 4
SWIGLU_ALPHA = 1.702
SWIGLU_LIMIT = 7.0
RMS_EPS = 1e-5
N_MOD = 6

LANES = 128
SUBLANES = 8
VMEM_LIMIT_BYTES = 56 * 1024 * 1024

ADA_TN = 1024
PROJ_TN = 1024
PROJ_TM = 512
SCAN_TC = 64
SCAN_SLAB = 256
SCAN_COLS = 512
SCAN_UNROLL = 4
RANK_TM = 512
MOE_SUB = 512
MOE_GROUP = 512
SLOT_BITS = 17
SLOT_MASK = (1 << SLOT_BITS) - 1
MIX_TM = 256
COMBINE_TM = 128
KMEAN_BLOCKS_PER_STEP = 8

NEG = -1e30


def _cparams(sem, vmem=VMEM_LIMIT_BYTES):
    return pltpu.CompilerParams(dimension_semantics=sem, vmem_limit_bytes=vmem)


def _dot(a, b):
    return jnp.dot(a, b, preferred_element_type=F32)


def _dot_nt(a, b, precision=None):
    return lax.dot_general(a, b, (((1,), (1,)), ((), ())), precision=precision,
                           preferred_element_type=F32)


def _sigmoid(x):
    return 1.0 / (1.0 + jnp.exp(-x))


def _rmsnorm(x, g):
    return (x * lax.rsqrt(jnp.mean(x * x, axis=-1, keepdims=True) + RMS_EPS)) * g


def _ada_kernel(c_ref, w_ref, b_ref, o_ref):
    c = c_ref[...]
    s = (c * _sigmoid(c)).astype(BF16)
    o_ref[...] = _dot(s, w_ref[...].astype(BF16)) + b_ref[...]


def _ada(c, w, b):
    r, d = c.shape
    m = w.shape[1]
    return pl.pallas_call(
        _ada_kernel,
        grid=(m // ADA_TN,),
        in_specs=[pl.BlockSpec((r, d), lambda j: (0, 0)),
                  pl.BlockSpec((d, ADA_TN), lambda j: (0, j)),
                  pl.BlockSpec((1, ADA_TN), lambda j: (0, j))],
        out_specs=pl.BlockSpec((r, ADA_TN), lambda j: (0, j)),
        out_shape=jax.ShapeDtypeStruct((r, m), F32),
        compiler_params=_cparams(("parallel",)),
        name="ada_mod",
    )(c, w, b)


def _inproj_kernel(x_ref, sc_ref, sh_ref, g_ref, w_ref,
                   q_ref, k_ref, v_ref, xs_ref, ga_ref, gb_ref, u_scr):
    j = pl.program_id(1)

    @pl.when(j == 0)
    def _():
        u = _rmsnorm(x_ref[...], g_ref[...]) * (1.0 + sc_ref[0]) + sh_ref[0]
        u_scr[...] = u.astype(BF16)

    r = _dot(u_scr[...], w_ref[...])
    for jj, ref in ((0, q_ref), (1, k_ref), (2, v_ref), (3, xs_ref)):
        @pl.when(j == jj)
        def _(ref=ref):
            ref[...] = r

    @pl.when((j == 4) | (j == 5))
    def _():
        ga_ref[...] = r.astype(BF16)

    @pl.when(j >= 6)
    def _():
        gb_ref[...] = r.astype(BF16)


def _inproj(x, sc, sh, g, w_bf, tm, tiles_per_group):
    n, d = x.shape
    tn = PROJ_TN
    assert w_bf.shape[1] == 8 * tn and d == 2 * tn, "projection layout: q|k|v|ssm each one tile, gates two"
    r = sc.shape[1]
    mod_spec = pl.BlockSpec((1, r, d), lambda i, j: (i // tiles_per_group, 0, 0))
    row = lambda i, j: (i, 0)
    outs = (
        jax.ShapeDtypeStruct((n, tn), F32),
        jax.ShapeDtypeStruct((n, tn), F32),
        jax.ShapeDtypeStruct((n, tn), F32),
        jax.ShapeDtypeStruct((n, tn), F32),
        jax.ShapeDtypeStruct((n, 2 * tn), BF16),
        jax.ShapeDtypeStruct((n, 2 * tn), BF16),
    )
    return pl.pallas_call(
        _inproj_kernel,
        grid=(n // tm, 8),
        in_specs=[pl.BlockSpec((tm, d), row), mod_spec, mod_spec,
                  pl.BlockSpec((1, d), lambda i, j: (0, 0)),
                  pl.BlockSpec((d, tn), lambda i, j: (0, j))],
        out_specs=(pl.BlockSpec((tm, tn), row), pl.BlockSpec((tm, tn), row),
                   pl.BlockSpec((tm, tn), row), pl.BlockSpec((tm, tn), row),
                   pl.BlockSpec((tm, tn), lambda i, j: (i, jnp.clip(j - 4, 0, 1))),
                   pl.BlockSpec((tm, tn), lambda i, j: (i, jnp.clip(j - 6, 0, 1)))),
        out_shape=outs,
        scratch_shapes=[pltpu.VMEM((tm, d), BF16)],
        compiler_params=_cparams(("parallel", "arbitrary")),
        name="in_proj",
    )(x, sc, sh, g, w_bf)


def _moba_prompt_kernel(q_ref, k_ref, v_ref, o_ref, kb_scr, vt_scr, km_scr, *, nb, scale):
    qi = pl.program_id(2)
    blk = MOBA_BLOCK
    dh = q_ref.shape[-1]

    @pl.when(qi == 0)
    def _():
        k = k_ref[...]
        km_scr[...] = jnp.sum(k.reshape(nb, blk, dh), axis=1) * (1.0 / blk)
        for j in range(nb):
            kb_scr[j] = k[j * blk:(j + 1) * blk].astype(BF16)
            vt_scr[j] = v_ref[j * blk:(j + 1) * blk, :].T.astype(BF16)

    q = q_ref[...]
    gate_t = _dot_nt(km_scr[...], q, precision=HIGHEST)
    jidx = lax.broadcasted_iota(I32, gate_t.shape, 0)
    cnt = jnp.zeros(gate_t.shape, I32)
    for jp in range(nb):
        row = gate_t[jp:jp + 1, :]
        beats = (row > gate_t) | ((row == gate_t) & (jp < jidx))
        cnt = cnt + jnp.where(jp < qi, jnp.where(beats, 1, 0), 0)
    sel = jnp.where((cnt < MOBA_TOPK) & (jidx < qi), 1.0, 0.0)

    qb = q.astype(BF16)
    causal = (lax.broadcasted_iota(I32, (blk, blk), 0) <= lax.broadcasted_iota(I32, (blk, blk), 1))

    def attend(own):
        blocks = []
        m = None
        for j in range(own + 1):
            s = _dot_nt(kb_scr[j], qb) * scale
            s = jnp.where(causal if j == own else sel[j:j + 1, :] > 0.5, s, NEG)
            blocks.append(s)
            mj = jnp.max(s, axis=0, keepdims=True)
            m = mj if m is None else jnp.maximum(m, mj)
        l = jnp.zeros((1, blk), F32)
        acc = jnp.zeros((dh, blk), F32)
        for j in range(own + 1):
            p = jnp.exp(blocks[j] - m)
            l = l + jnp.sum(p, axis=0, keepdims=True)
            acc = acc + _dot(vt_scr[j], p.astype(BF16))
        o_ref[...] = (acc / l).T.astype(o_ref.dtype)

    for own in range(nb):
        @pl.when(qi == own)
        def _(own=own):
            attend(own)


def _moba_prompt(q, k, v, n_heads):
    b, t, w = q.shape
    dh = w // n_heads
    assert t % MOBA_BLOCK == 0 and dh == LANES
    nb = t // MOBA_BLOCK
    kern = functools.partial(_moba_prompt_kernel, nb=nb, scale=dh ** -0.5)
    kv_spec = pl.BlockSpec((None, t, dh), lambda bi, h, qi: (bi, 0, h))
    q_spec = pl.BlockSpec((None, MOBA_BLOCK, dh), lambda bi, h, qi: (bi, qi, h))
    return pl.pallas_call(
        kern,
        grid=(b, n_heads, nb),
        in_specs=[q_spec, kv_spec, kv_spec],
        out_specs=q_spec,
        out_shape=jax.ShapeDtypeStruct((b, t, w), BF16),
        scratch_shapes=[pltpu.VMEM((nb, MOBA_BLOCK, dh), BF16), pltpu.VMEM((nb, dh, MOBA_BLOCK), BF16),
                        pltpu.VMEM((nb, dh), F32)],
        compiler_params=_cparams(("parallel", "parallel", "arbitrary")),
        name="moba_prompt",
    )(q, k, v)


def _kmean_kernel(pt_ref, *refs, ppb, bps):
    del pt_ref
    o_ref = refs[ppb * bps]
    j = pl.program_id(1)
    for n in range(bps):
        s = jnp.sum(refs[n * ppb][...], axis=0)
        for p in range(1, ppb):
            s = s + jnp.sum(refs[n * ppb + p][...], axis=0)
        o_ref[pl.ds(j * bps + n, 1)] = (s * (1.0 / MOBA_BLOCK))[None]


def _cache_block_means(cache_k, layer, page_table, n_blocks):
    _, _, page, h, dh = cache_k.shape
    db = page_table.shape[0]
    ppb = MOBA_BLOCK // page
    bps = math.gcd(n_blocks, KMEAN_BLOCKS_PER_STEP)
    page_specs = [
        pl.BlockSpec((None, None, page, h, dh),
                     functools.partial(lambda b, j, pt, p: (layer, pt[b, j * ppb * bps + p], 0, 0, 0), p=p))
        for p in range(ppb * bps)]
    return pl.pallas_call(
        functools.partial(_kmean_kernel, ppb=ppb, bps=bps),
        grid_spec=pltpu.PrefetchScalarGridSpec(
            num_scalar_prefetch=1,
            grid=(db, n_blocks // bps),
            in_specs=page_specs,
            out_specs=pl.BlockSpec((None, n_blocks, h, dh), lambda b, j, pt: (b, 0, 0, 0))),
        out_shape=jax.ShapeDtypeStruct((db, n_blocks, h, dh), F32),
        compiler_params=_cparams(("parallel", "arbitrary")),
        name="cache_block_means",
    )(page_table, *([cache_k] * (ppb * bps)))


def _select_kernel(q_ref, km_ref, o_ref, *, n_heads, n_blocks):
    dh = km_ref.shape[-1]
    rows = []
    for h in range(n_heads):
        qh = q_ref[:, h * dh:(h + 1) * dh]
        rows.append(_dot_nt(qh, km_ref[:, h, :], precision=HIGHEST))
    g = jnp.concatenate(rows, axis=0)
    lane = lax.broadcasted_iota(I32, g.shape, 1)
    out_lane = lax.broadcasted_iota(I32, o_ref.shape, 1)
    out = jnp.zeros(o_ref.shape, I32)
    for t in range(MOBA_TOPK):
        m = jnp.max(g, axis=1, keepdims=True)
        idx = jnp.min(jnp.where(g == m, lane, n_blocks), axis=1, keepdims=True)
        out = jnp.where(out_lane == t, idx, out)
        g = jnp.where(lane == idx, -jnp.inf, g)
    o_ref[...] = out


def _select_blocks(q_pad, kmean, n_heads):
    db, qp, w = q_pad.shape
    n_blocks = kmean.shape[1]
    dh = w // n_heads
    return pl.pallas_call(
        functools.partial(_select_kernel, n_heads=n_heads, n_blocks=n_blocks),
        grid=(db,),
        in_specs=[pl.BlockSpec((None, qp, w), lambda b: (b, 0, 0)),
                  pl.BlockSpec((None, n_blocks, n_heads, dh), lambda b: (b, 0, 0, 0))],
        out_specs=pl.BlockSpec((None, n_heads * qp, LANES), lambda b: (b, 0, 0)),
        out_shape=jax.ShapeDtypeStruct((db, n_heads * qp, LANES), I32),
        compiler_params=_cparams(("parallel",)),
        name="moba_select",
    )(q_pad, kmean)


def _sample_attn_kernel(sel_ref, pt_ref, q_ref, kn_ref, vn_ref, ck_hbm, cv_hbm, o_ref,
                        kbuf, vbuf, sems, *, layer, n_heads, n_q, qp, page, scale):
    b = pl.program_id(0)
    ppb = MOBA_BLOCK // page
    dh = kbuf.shape[-1]
    per_head = n_q * MOBA_TOPK * ppb
    span = MOBA_TOPK * MOBA_BLOCK

    def copies(i):
        h = i // per_head
        rem = i % per_head
        qs = rem // ppb
        p = rem % ppb
        q = qs // MOBA_TOPK
        s = qs % MOBA_TOPK
        blk = sel_ref[b, (h * qp + q) * MOBA_TOPK + s]
        pg = pt_ref[b, blk * ppb + p]
        dst = pl.ds(qs * MOBA_BLOCK + p * page, page)
        return (pltpu.make_async_copy(ck_hbm.at[layer, pg, :, h, :], kbuf.at[h, dst, :], sems.at[0]),
                pltpu.make_async_copy(cv_hbm.at[layer, pg, :, h, :], vbuf.at[h, dst, :], sems.at[1]))

    def start(i, c):
        ck, cv = copies(i)
        ck.start()
        cv.start()
        return c

    def wait(i, c):
        ck, cv = copies(i)
        ck.wait()
        cv.wait()
        return c

    lax.fori_loop(0, n_heads * per_head, start, 0)
    lax.fori_loop(0, n_heads * per_head, wait, 0)

    n_keys = n_q * span
    row = lax.broadcasted_iota(I32, (qp, n_keys), 0)
    col = lax.broadcasted_iota(I32, (qp, n_keys), 1)
    own = (col >= row * span) & (col < (row + 1) * span)
    rown = lax.broadcasted_iota(I32, (qp, qp), 0)
    coln = lax.broadcasted_iota(I32, (qp, qp), 1)
    causal = (coln <= rown) & (coln < n_q)
    for h in range(n_heads):
        hs = slice(h * dh, (h + 1) * dh)
        qh = q_ref[:, hs].astype(BF16)
        s = jnp.where(own, _dot_nt(qh, kbuf[h].astype(BF16)) * scale, NEG)
        sn = jnp.where(causal, _dot_nt(qh, kn_ref[:, hs].astype(BF16)) * scale, NEG)
        m = jnp.maximum(jnp.max(s, axis=1, keepdims=True), jnp.max(sn, axis=1, keepdims=True))
        p = jnp.exp(s - m)
        pn = jnp.exp(sn - m)
        l = jnp.sum(p, axis=1, keepdims=True) + jnp.sum(pn, axis=1, keepdims=True)
        o = _dot(p.astype(BF16), vbuf[h].astype(BF16)) + _dot(pn.astype(BF16), vn_ref[:, hs].astype(BF16))
        o_ref[:, hs] = o / l


def _sample_attention(q_pad, k_pad, v_pad, sel_flat, page_table, cache_k, cache_v, layer, n_heads, n_q):
    db, qp, w = q_pad.shape
    dh = w // n_heads
    page = cache_k.shape[2]
    n_keys = n_q * MOBA_TOPK * MOBA_BLOCK
    kern = functools.partial(_sample_attn_kernel, layer=layer, n_heads=n_heads, n_q=n_q, qp=qp,
                             page=page, scale=dh ** -0.5)
    row_spec = pl.BlockSpec((None, qp, w), lambda b, sel, pt: (b, 0, 0))
    return pl.pallas_call(
        kern,
        grid_spec=pltpu.PrefetchScalarGridSpec(
            num_scalar_prefetch=2,
            grid=(db,),
            in_specs=[row_spec, row_spec, row_spec,
                      pl.BlockSpec(memory_space=pl.ANY), pl.BlockSpec(memory_space=pl.ANY)],
            out_specs=row_spec,
            scratch_shapes=[pltpu.VMEM((n_heads, n_keys, dh), F32),
                            pltpu.VMEM((n_heads, n_keys, dh), F32),
                            pltpu.SemaphoreType.DMA((2,))]),
        out_shape=jax.ShapeDtypeStruct((db, qp, w), F32),
        compiler_params=_cparams(("arbitrary",)),
        name="moba_sample",
    )(sel_flat, page_table, q_pad, k_pad, v_pad, cache_k, cache_v)


def _s5_param_kernel(are_ref, aim_ref, ldt_ref, bre_ref, bim_ref,
                     o_are, o_aim, o_bre, o_bim):
    a_re = are_ref[...]
    a_im = aim_ref[...]
    dt = jnp.exp(ldt_ref[...])
    mag = jnp.exp(a_re * dt)
    ab_re = mag * jnp.cos(a_im * dt)
    ab_im = mag * jnp.sin(a_im * dt)
    den = a_re * a_re + a_im * a_im
    nr = ab_re - 1.0
    ni = ab_im
    c_re = (nr * a_re + ni * a_im) / den
    c_im = (ni * a_re - nr * a_im) / den
    b_re = bre_ref[...]
    b_im = bim_ref[...]
    o_are[...] = ab_re
    o_aim[...] = ab_im
    o_bre[...] = c_re * b_re - c_im * b_im
    o_bim[...] = c_re * b_im + c_im * b_re


def _s5_params(a_re, a_im, log_dt, b_re, b_im):
    g, p = a_re.shape
    c = b_re.shape[-1]
    bt_re = jnp.swapaxes(b_re, 1, 2)
    bt_im = jnp.swapaxes(b_im, 1, 2)
    sds = jax.ShapeDtypeStruct
    ab_re, ab_im, bb_re, bb_im = pl.pallas_call(
        _s5_param_kernel,
        out_shape=(sds((g, 1, p), F32), sds((g, 1, p), F32), sds((g, c, p), F32), sds((g, c, p), F32)),
        name="s5_params",
    )(a_re.reshape(g, 1, p), a_im.reshape(g, 1, p), log_dt.reshape(g, 1, 1), bt_re, bt_im)
    return ab_re.reshape(g, p), ab_im.reshape(g, p), bb_re, bb_im


def _block_diag_slabs(m, groups_per_slab):
    g, r, c = m.shape
    n = g // groups_per_slab
    eye = jnp.eye(groups_per_slab, dtype=m.dtype)
    m4 = m.reshape(n, groups_per_slab, r, c)
    out = jnp.einsum("ngrc,gh->ngrhc", m4, eye)
    return out.reshape(n, groups_per_slab * r, groups_per_slab * c)


def _s5_kernel(xs_ref, s0re_ref, s0im_ref, are_ref, aim_ref, bre_ref, bim_ref, cre_ref, cim_ref, d_ref,
               gy_ref, fre_ref, fim_ref, u_t, y_t, sre, sim, st_re, st_im):
    c = pl.program_id(0)
    n_batch, rpb, w = xs_ref.shape
    n_slab, slab_in, slab_st = bre_ref.shape
    n_state = sre.shape[1]
    in_tiles = w // LANES
    tiles_per_slab = slab_in // LANES

    @pl.when(c == 0)
    def _():
        st_re[...] = s0re_ref[...]
        st_im[...] = s0im_ref[...]

    for b in range(n_batch):
        for ct in range(in_tiles):
            u_t[ct, pl.ds(b, rpb, stride=n_batch), :] = xs_ref[b, :, ct * LANES:(ct + 1) * LANES]

    for k in range(n_slab):
        u = jnp.concatenate([u_t[k * tiles_per_slab + i] for i in range(tiles_per_slab)], axis=1).astype(BF16)
        sre[:, k * slab_st:(k + 1) * slab_st] = _dot(u, bre_ref[k])
        sim[:, k * slab_st:(k + 1) * slab_st] = _dot(u, bim_ref[k])

    for bg in range(n_batch // SUBLANES):
        brows = slice(bg * SUBLANES, (bg + 1) * SUBLANES)
        for cs in range(n_state // SCAN_COLS):
            cols = slice(cs * SCAN_COLS, (cs + 1) * SCAN_COLS)
            a_re = jnp.broadcast_to(are_ref[:, cols], (SUBLANES, SCAN_COLS))
            a_im = jnp.broadcast_to(aim_ref[:, cols], (SUBLANES, SCAN_COLS))

            def step(t, carry, cols=cols, bg=bg, a_re=a_re, a_im=a_im):
                s_re, s_im = carry
                rows = pl.ds(pl.multiple_of(t * n_batch + bg * SUBLANES, SUBLANES), SUBLANES)
                n_re = a_re * s_re - a_im * s_im + sre[rows, cols]
                n_im = a_re * s_im + a_im * s_re + sim[rows, cols]
                sre[rows, cols] = n_re
                sim[rows, cols] = n_im
                return n_re, n_im

            f_re, f_im = lax.fori_loop(0, rpb, step, (st_re[brows, cols], st_im[brows, cols]),
                                       unroll=SCAN_UNROLL)
            st_re[brows, cols] = f_re
            st_im[brows, cols] = f_im

    fre_ref[...] = st_re[...]
    fim_ref[...] = st_im[...]

    for k in range(n_slab):
        scols = slice(k * slab_st, (k + 1) * slab_st)
        y = _dot(sre[:, scols].astype(BF16), cre_ref[k]) - _dot(sim[:, scols].astype(BF16), cim_ref[k])
        for i in range(tiles_per_slab):
            ct = k * tiles_per_slab + i
            yy = y[:, i * LANES:(i + 1) * LANES] + d_ref[:, ct * LANES:(ct + 1) * LANES] * u_t[ct]
            inner = math.sqrt(2.0 / math.pi) * (yy + 0.044715 * (yy * yy * yy))
            y_t[ct] = 0.5 * yy * (1.0 + jnp.tanh(inner))

    for b in range(n_batch):
        for ct in range(in_tiles):
            gy_ref[b, :, ct * LANES:(ct + 1) * LANES] = y_t[ct, pl.ds(b, rpb, stride=n_batch), :]


def _s5(xs3, s0_re, s0_im, abar_re, abar_im, bre_bd, bim_bd, cre_bd, cim_bd, d_row, rows_per_batch):
    n_batch, t, w = xs3.shape
    rpb = rows_per_batch
    n_state = abar_re.shape[1]
    rows = n_batch * rpb
    assert t % rpb == 0 and n_batch % SUBLANES == 0 and n_state % SCAN_COLS == 0 and w % LANES == 0
    full = lambda a: pl.BlockSpec(a.shape, lambda c: (0,) * a.ndim)
    st_spec = pl.BlockSpec((n_batch, n_state), lambda c: (0, 0))
    x_spec = pl.BlockSpec((n_batch, rpb, w), lambda c: (0, c, 0))
    sds = jax.ShapeDtypeStruct
    return pl.pallas_call(
        _s5_kernel,
        grid=(t // rpb,),
        in_specs=[x_spec, st_spec, st_spec, full(abar_re), full(abar_im), full(bre_bd), full(bim_bd),
                  full(cre_bd), full(cim_bd), full(d_row)],
        out_specs=(x_spec, st_spec, st_spec),
        out_shape=(sds((n_batch, t, w), F32), sds((n_batch, n_state), F32), sds((n_batch, n_state), F32)),
        scratch_shapes=[pltpu.VMEM((w // LANES, rows, LANES), F32), pltpu.VMEM((w // LANES, rows, LANES), F32),
                        pltpu.VMEM((rows, n_state), F32), pltpu.VMEM((rows, n_state), F32),
                        pltpu.VMEM((n_batch, n_state), F32), pltpu.VMEM((n_batch, n_state), F32)],
        compiler_params=_cparams(("arbitrary",)),
        name="s5_scan",
    )(xs3, s0_re, s0_im, abar_re, abar_im, bre_bd, bim_bd, cre_bd, cim_bd, d_row)


def _merge_kernel(att_ref, gy_ref, ga_ref, gb_ref, wbr_ref, wv_ref, wg_ref, o_ref):
    y_att = _dot(att_ref[...], wbr_ref[...])
    gy = gy_ref[...].astype(BF16)
    y_ssm = _dot(gy, wv_ref[...]) * _sigmoid(_dot(gy, wg_ref[...]))
    merged = _sigmoid(ga_ref[...].astype(F32)) * y_att + _sigmoid(gb_ref[...].astype(F32)) * y_ssm
    o_ref[...] = merged.astype(o_ref.dtype)


def _merge(att, gy, ga, gb, wbr, wv, wg, tm):
    n, aw = att.shape
    sw = gy.shape[1]
    d = ga.shape[1]
    row = lambda i: (i, 0)
    const = lambda i: (0, 0)
    return pl.pallas_call(
        _merge_kernel,
        grid=(n // tm,),
        in_specs=[pl.BlockSpec((tm, aw), row), pl.BlockSpec((tm, sw), row),
                  pl.BlockSpec((tm, d), row), pl.BlockSpec((tm, d), row),
                  pl.BlockSpec((aw, d), const), pl.BlockSpec((sw, d), const), pl.BlockSpec((sw, d), const)],
        out_specs=pl.BlockSpec((tm, d), row),
        out_shape=jax.ShapeDtypeStruct((n, d), BF16),
        compiler_params=_cparams(("parallel",)),
        name="branch_merge",
    )(att, gy, ga, gb, wbr, wv, wg)


def _outproj_kernel(m_ref, x_ref, g1_ref, sc_ref, sh_ref, ng_ref, wo_ref, wr_ref, br_ref,
                    h_ref, u_ref, ti_ref, gt_ref):
    h = x_ref[...] + g1_ref[0] * _dot(m_ref[...], wo_ref[...])
    h_ref[...] = h
    u = _rmsnorm(h, ng_ref[...]) * (1.0 + sc_ref[0]) + sh_ref[0]
    u_ref[...] = u
    logits = _dot_nt(wr_ref[...], u, precision=HIGHEST) + br_ref[...]
    n_exp = logits.shape[0]
    eidx = lax.broadcasted_iota(I32, logits.shape, 0)
    vals, idxs = [], []
    for _ in range(MOE_TOPK):
        m = jnp.max(logits, axis=0, keepdims=True)
        idx = jnp.min(jnp.where(logits == m, eidx, n_exp), axis=0, keepdims=True)
        vals.append(m)
        idxs.append(idx)
        logits = jnp.where(eidx == idx, -jnp.inf, logits)
    es = [jnp.exp(v - vals[0]) for v in vals]
    tot = es[0]
    for e in es[1:]:
        tot = tot + e
    ti_ref[...] = jnp.concatenate(idxs, axis=0)
    gt_ref[...] = jnp.concatenate([e / tot for e in es], axis=0)


def _outproj(merged, x, g1, sc, sh, ng, wo_bf, wr_t, br_col, tm, tiles_per_group):
    n, d = x.shape
    n_exp = wr_t.shape[0]
    r = sc.shape[1]
    mod_spec = pl.BlockSpec((1, r, d), lambda i: (i // tiles_per_group, 0, 0))
    row = lambda i: (i, 0)
    const = lambda i: (0, 0)
    sds = jax.ShapeDtypeStruct
    return pl.pallas_call(
        _outproj_kernel,
        grid=(n // tm,),
        in_specs=[pl.BlockSpec((tm, d), row), pl.BlockSpec((tm, d), row), mod_spec, mod_spec, mod_spec,
                  pl.BlockSpec((1, d), const), pl.BlockSpec((d, d), const),
                  pl.BlockSpec((n_exp, d), const), pl.BlockSpec((n_exp, 1), const)],
        out_specs=(pl.BlockSpec((tm, d), row), pl.BlockSpec((tm, d), row),
                   pl.BlockSpec((MOE_TOPK, tm), lambda i: (0, i)),
                   pl.BlockSpec((MOE_TOPK, tm), lambda i: (0, i))),
        out_shape=(sds((n, d), F32), sds((n, d), F32), sds((MOE_TOPK, n), I32), sds((MOE_TOPK, n), F32)),
        compiler_params=_cparams(("parallel",)),
        name="out_proj_router",
    )(merged, x, g1, sc, sh, ng, wo_bf, wr_t, br_col)


def _rank_kernel(ti_ref, rank_ref, cnt_ref, carry, *, n_exp):
    i = pl.program_id(0)
    tm = ti_ref.shape[1]

    @pl.when(i == 0)
    def _():
        carry[...] = jnp.zeros(carry.shape, F32)

    ti = ti_ref[...]
    eidx = lax.broadcasted_iota(I32, (n_exp, tm), 0)
    tri = jnp.where(lax.broadcasted_iota(I32, (tm, tm), 0) <= lax.broadcasted_iota(I32, (tm, tm), 1),
                    1.0, 0.0).astype(BF16)
    base = carry[...]
    ranks = []
    for k in range(MOE_TOPK):
        hit = eidx == ti[k:k + 1, :]
        onehot = jnp.where(hit, 1.0, 0.0)
        incl = _dot(onehot.astype(BF16), tri)
        ranks.append(jnp.sum(jnp.where(hit, base + incl - 1.0, 0.0), axis=0, keepdims=True))
        base = base + jnp.sum(onehot, axis=1, keepdims=True)
    carry[...] = base
    rank_ref[...] = jnp.concatenate(ranks, axis=0).astype(I32)
    cnt_ref[...] = jnp.broadcast_to(base, cnt_ref.shape)


def _expert_ranks(top_i, n_exp):
    k, n = top_i.shape
    rank, cnt = pl.pallas_call(
        functools.partial(_rank_kernel, n_exp=n_exp),
        grid=(n // RANK_TM,),
        in_specs=[pl.BlockSpec((k, RANK_TM), lambda i: (0, i))],
        out_specs=(pl.BlockSpec((k, RANK_TM), lambda i: (0, i)),
                   pl.BlockSpec((n_exp, LANES), lambda i: (0, 0))),
        out_shape=(jax.ShapeDtypeStruct((k, n), I32), jax.ShapeDtypeStruct((n_exp, LANES), F32)),
        scratch_shapes=[pltpu.VMEM((n_exp, 1), F32)],
        compiler_params=_cparams(("arbitrary",)),
        name="expert_ranks",
    )(top_i)
    return rank, cnt[:, 0].astype(I32)


def _wprep_kernel(w_ref, perm_ref, o_ref):
    o_ref[...] = _dot(w_ref[...].astype(BF16), perm_ref[...]).astype(BF16)


def _prep_gate_up(w_gu):
    n_exp, d, ff2 = w_gu.shape
    g = MOE_GROUP
    col = jnp.arange(g, dtype=I32)
    target = jnp.where(col % 2 == 0, col // 2, g // 2 + col // 2)
    perm = (target[:, None] == col[None, :]).astype(BF16)
    return pl.pallas_call(
        _wprep_kernel,
        grid=(n_exp, ff2 // g),
        in_specs=[pl.BlockSpec((None, d, g), lambda e, j: (e, 0, j)),
                  pl.BlockSpec((g, g), lambda e, j: (0, 0))],
        out_specs=pl.BlockSpec((None, d, g), lambda e, j: (e, 0, j)),
        out_shape=jax.ShapeDtypeStruct((n_exp, d, ff2), BF16),
        compiler_params=_cparams(("parallel", "parallel")),
        name="moe_weight_prep",
    )(w_gu, perm)


def _group_bias(b_gu):
    n_exp, ff2 = b_gu.shape
    b = b_gu.reshape(n_exp, ff2 // MOE_GROUP, MOE_GROUP // 2, 2)
    return jnp.swapaxes(b, 2, 3).reshape(n_exp, 1, ff2)


def _expert_kernel(ss_ref, ns_ref, pk_ref, bgu_ref, bd_ref, u_hbm, wgu_hbm, wd_hbm, slots_hbm,
                   wgu_v, wd_v, x_st, xb_scr, act_scr, o_st, sem_w, sem_g, sem_s, *, n_exp):
    e = pl.program_id(0)
    sub0 = ss_ref[e]
    nsub = ns_ref[e]
    sub = MOE_SUB
    n_grp = wgu_v.shape[1] // MOE_GROUP
    half = MOE_GROUP // 2
    chunk = sub // n_grp

    def gather(row, slot, r):
        tok = lax.shift_right_logical(pk_ref[row], jnp.int32(SLOT_BITS))
        return pltpu.make_async_copy(u_hbm.at[pl.ds(tok, 1), :], x_st.at[slot, pl.ds(r, 1), :], sem_g.at[slot])

    def scatter(row, slot, r):
        dst = pk_ref[row] & SLOT_MASK
        return pltpu.make_async_copy(o_st.at[slot, pl.ds(r, 1), :], slots_hbm.at[pl.ds(dst, 1), :],
                                     sem_s.at[slot])

    def wait_rows(make, slot):
        def body(r, c):
            make(0, slot, 0).wait()
            return c
        lax.fori_loop(0, sub, body, 0, unroll=8)

    def start_rows(make, base, slot):
        def body(r, c):
            make(base + r, slot, r).start()
            return c
        lax.fori_loop(0, sub, body, 0, unroll=8)

    def w_copies():
        gu = [pltpu.make_async_copy(wgu_hbm.at[e, :, pl.ds(g * MOE_GROUP, MOE_GROUP)],
                                    wgu_v.at[:, pl.ds(g * MOE_GROUP, MOE_GROUP)], sem_w.at[g])
              for g in range(n_grp)]
        return gu, pltpu.make_async_copy(wd_hbm.at[e], wd_v, sem_w.at[n_grp])

    def block(gsub, first):
        slot = gsub % 2
        other = 1 - slot
        prev = jnp.where(gsub == 0, used_sub, gsub - 1)
        gu_copies, d_copy = w_copies()
        wait_rows(gather, slot)
        xb_scr[...] = x_st[slot].astype(BF16)
        for g in range(n_grp):
            cols = slice(g * MOE_GROUP, (g + 1) * MOE_GROUP)
            if first:
                gu_copies[g].wait()
            h = _dot(xb_scr[...], wgu_v[:, cols]) + bgu_ref[:, cols]
            h_glu = jnp.minimum(h[:, :half], SWIGLU_LIMIT)
            h_lin = jnp.clip(h[:, half:], -SWIGLU_LIMIT, SWIGLU_LIMIT)
            act = h_glu * _sigmoid(SWIGLU_ALPHA * h_glu) * (h_lin + 1.0)
            act_scr[:, g * half:(g + 1) * half] = act.astype(BF16)
            for r in range(g * chunk, (g + 1) * chunk):
                gather((gsub + 1) * sub + r, other, r).start()
                scatter(prev * sub + r, other, r).start()
        if first:
            d_copy.wait()
        o_st[slot] = _dot(act_scr[...], wd_v[...]) + bd_ref[...]
        wait_rows(scatter, other)

    used_sub = ss_ref[n_exp]

    @pl.when(e == 0)
    def _():
        o_st[1] = jnp.zeros(o_st.shape[1:], o_st.dtype)
        start_rows(gather, 0, 0)

    @pl.when(nsub > 0)
    def _():
        gu_copies, d_copy = w_copies()
        for c in gu_copies:
            c.start()
        d_copy.start()
        block(sub0, True)

        def body(s, c):
            block(sub0 + s, False)
            return c

        lax.fori_loop(1, nsub, body, 0)

    @pl.when(e == n_exp - 1)
    def _():
        last = used_sub - 1
        lslot = last % 2
        wait_rows(gather, 1 - lslot)
        start_rows(scatter, last * sub, lslot)
        wait_rows(scatter, lslot)
        o_st[0] = jnp.zeros(o_st.shape[1:], o_st.dtype)

        def tail(i, c):
            start_rows(scatter, (used_sub + 1 + i) * sub, 0)
            wait_rows(scatter, 0)
            return c

        lax.fori_loop(0, ns_ref[n_exp] - 1, tail, 0)


def _experts(u_all, sub_start, n_sub, packed, w_gu_bf, w_d_bf, b_gu, b_d, n_rows):
    n, d = u_all.shape
    n_exp, _, ff2 = w_gu_bf.shape
    ff = ff2 // 2
    sub = MOE_SUB
    assert ff2 % MOE_GROUP == 0 and sub % (ff2 // MOE_GROUP) == 0
    return pl.pallas_call(
        functools.partial(_expert_kernel, n_exp=n_exp),
        grid_spec=pltpu.PrefetchScalarGridSpec(
            num_scalar_prefetch=3,
            grid=(n_exp,),
            in_specs=[pl.BlockSpec((None, 1, ff2), lambda e, ss, ns, pk: (e, 0, 0)),
                      pl.BlockSpec((None, 1, d), lambda e, ss, ns, pk: (e, 0, 0)),
                      pl.BlockSpec(memory_space=pl.ANY), pl.BlockSpec(memory_space=pl.ANY),
                      pl.BlockSpec(memory_space=pl.ANY)],
            out_specs=pl.BlockSpec(memory_space=pl.ANY),
            scratch_shapes=[pltpu.VMEM((d, ff2), BF16), pltpu.VMEM((ff, d), BF16),
                            pltpu.VMEM((2, sub, d), F32), pltpu.VMEM((sub, d), BF16),
                            pltpu.VMEM((sub, ff), BF16), pltpu.VMEM((2, sub, d), F32),
                            pltpu.SemaphoreType.DMA((ff2 // MOE_GROUP + 1,)), pltpu.SemaphoreType.DMA((2,)),
                            pltpu.SemaphoreType.DMA((2,))]),
        out_shape=jax.ShapeDtypeStruct((n_rows, d), F32),
        compiler_params=_cparams(("arbitrary",)),
        name="moe_experts",
    )(sub_start, n_sub, packed, b_gu, b_d, u_all, w_gu_bf, w_d_bf)


def _combine_kernel(*refs):
    slot_refs = refs[:MOE_TOPK]
    gt_ref, h_ref, g2_ref, fg_ref, y_ref = refs[MOE_TOPK:]
    gt = gt_ref[...]
    y = gt[:, 0:1] * slot_refs[0][...]
    for k in range(1, MOE_TOPK):
        y = y + gt[:, k:k + 1] * slot_refs[k][...]
    h = h_ref[...] + g2_ref[0] * y
    y_ref[...] = _rmsnorm(h, fg_ref[...])


def _combine(slots, n_all, tok0, gates_t, h, g2, final_g, tm, tiles_per_group):
    n, d = h.shape
    r = g2.shape[1]
    assert n_all % tm == 0 and tok0 % tm == 0
    row = lambda i: (i, 0)
    slot_specs = [pl.BlockSpec((tm, d), functools.partial(lambda i, k: ((k * n_all + tok0) // tm + i, 0), k=k))
                  for k in range(MOE_TOPK)]
    return pl.pallas_call(
        _combine_kernel,
        grid=(n // tm,),
        in_specs=slot_specs + [pl.BlockSpec((tm, MOE_TOPK), row), pl.BlockSpec((tm, d), row),
                               pl.BlockSpec((1, r, d), lambda i: (i // tiles_per_group, 0, 0)),
                               pl.BlockSpec((1, d), lambda i: (0, 0))],
        out_specs=pl.BlockSpec((tm, d), row),
        out_shape=jax.ShapeDtypeStruct((n, d), F32),
        compiler_params=_cparams(("parallel",)),
        name="moe_combine",
    )(*([slots] * MOE_TOPK), gates_t, h, g2, final_g)


def _mods(mod, rows_per_batch, per_row):
    b = mod.shape[0]
    parts = jnp.split(mod, N_MOD, axis=-1)
    if per_row:
        return [jnp.repeat(p, rows_per_batch, axis=0)[None] for p in parts]
    return [p.reshape(b, 1, -1) for p in parts]


def kernel(x_prompt, x_sample, cache_k, cache_v, state_ssm_re, state_ssm_im, page_table, c_prompt, c_sample,
           w_ada, b_ada, norm1_g, w_in, w_attn_br, ssm_a_re, ssm_a_im, ssm_log_dt, ssm_b_re, ssm_b_im,
           ssm_c_re, ssm_c_im, ssm_d, w_ssm_val, w_ssm_gate, w_out, norm2_g, w_router, b_router,
           w_gate_up, b_gate_up, w_down, b_down, final_g):
    depth = w_in.shape[0]
    bp, t, d = x_prompt.shape
    db, tq, _ = x_sample.shape
    _, n_pool, page, n_heads, dh = cache_k.shape
    aw = n_heads * dh
    n_groups, n_st = ssm_a_re.shape[1:]
    sw = n_groups * SSM_GROUP
    n_state = n_groups * n_st
    n_exp = w_router.shape[-1]
    past = page_table.shape[1] * page
    n_past_blocks = past // MOBA_BLOCK
    assert past % MOBA_BLOCK == 0 and MOBA_BLOCK % page == 0, "cached keys must fill whole MoBA blocks"
    assert n_past_blocks >= MOBA_TOPK and tq <= SUBLANES, "sample step: all selected blocks are past blocks"
    assert sw % SCAN_SLAB == 0 and t % SCAN_TC == 0 and t % PROJ_TM == 0

    np_tok = bp * t
    ns_tok = db * tq
    n_tok = np_tok + ns_tok
    qp = SUBLANES

    hp = x_prompt.reshape(np_tok, d)
    hs = x_sample.reshape(ns_tok, d)
    c_all = jnp.concatenate([c_prompt, c_sample], axis=0)

    outs = {k: [] for k in ("kp", "vp", "srp", "sip", "ks", "vs", "srs", "sis")}
    gps = SCAN_SLAB // SSM_GROUP
    for l in range(depth):
        mod = _ada(c_all, w_ada[l], b_ada[l][None])
        mod_p = _mods(mod[:bp], t, per_row=False)
        mod_s = _mods(mod[bp:], tq, per_row=True)

        w_in_bf = w_in[l].astype(BF16)
        wbr = w_attn_br[l].astype(BF16)
        wv = w_ssm_val[l].astype(BF16)
        wg = w_ssm_gate[l].astype(BF16)
        wo = w_out[l].astype(BF16)
        wr_t = w_router[l].T
        br_col = b_router[l][:, None]
        n1g = norm1_g[l][None]
        n2g = norm2_g[l][None]

        abar_re, abar_im, bbt_re, bbt_im = _s5_params(ssm_a_re[l], ssm_a_im[l], ssm_log_dt[l],
                                                      ssm_b_re[l], ssm_b_im[l])
        bre_bd = _block_diag_slabs(bbt_re, gps).astype(BF16)
        bim_bd = _block_diag_slabs(bbt_im, gps).astype(BF16)
        cre_bd = _block_diag_slabs(jnp.swapaxes(ssm_c_re[l], 1, 2), gps).astype(BF16)
        cim_bd = _block_diag_slabs(jnp.swapaxes(ssm_c_im[l], 1, 2), gps).astype(BF16)
        abar_re_row = abar_re.reshape(1, n_state)
        abar_im_row = abar_im.reshape(1, n_state)
        d_row = ssm_d[l].reshape(1, sw)

        tpg = t // PROJ_TM
        q, k, v, xs, ga, gb = _inproj(hp, mod_p[1], mod_p[0], n1g, w_in_bf, PROJ_TM, tpg)
        att = _moba_prompt(q.reshape(bp, t, aw), k.reshape(bp, t, aw), v.reshape(bp, t, aw), n_heads)
        zeros_state = jnp.zeros((bp, n_state), F32)
        gy, srp, sip = _s5(xs.reshape(bp, t, sw), zeros_state, zeros_state, abar_re_row, abar_im_row,
                           bre_bd, bim_bd, cre_bd, cim_bd, d_row, SCAN_TC)
        merged = _merge(att.reshape(np_tok, aw), gy.reshape(np_tok, sw), ga, gb, wbr, wv, wg, MIX_TM)
        hp_mid, u2p, tip, gtp = _outproj(merged, hp, mod_p[2], mod_p[4], mod_p[3], n2g, wo, wr_t, br_col,
                                         MIX_TM, t // MIX_TM)
        outs["kp"].append(k.reshape(bp, t, n_heads, dh))
        outs["vp"].append(v.reshape(bp, t, n_heads, dh))
        outs["srp"].append(srp.reshape(bp, n_groups, n_st))
        outs["sip"].append(sip.reshape(bp, n_groups, n_st))

        qs, ks, vs, xss, gas, gbs = _inproj(hs, mod_s[1], mod_s[0], n1g, w_in_bf, ns_tok, 1)
        pad_q = lambda a: jnp.pad(a.reshape(db, tq, aw), ((0, 0), (0, qp - tq), (0, 0)))
        q_pad, k_pad, v_pad = pad_q(qs), pad_q(ks), pad_q(vs)
        kmean = _cache_block_means(cache_k, l, page_table, n_past_blocks)
        sel = _select_blocks(q_pad, kmean.reshape(db, n_past_blocks, n_heads, dh), n_heads)
        sel_flat = sel[:, :, :MOBA_TOPK].reshape(db, n_heads * qp * MOBA_TOPK)
        att_s = _sample_attention(q_pad, k_pad, v_pad, sel_flat, page_table, cache_k, cache_v, l, n_heads, tq)
        att_s = att_s[:, :tq].reshape(ns_tok, aw).astype(BF16)
        gys, srs, sis = _s5(xss.reshape(db, tq, sw), state_ssm_re[l].reshape(db, n_state),
                            state_ssm_im[l].reshape(db, n_state), abar_re_row, abar_im_row,
                            bre_bd, bim_bd, cre_bd, cim_bd, d_row, tq)
        merged_s = _merge(att_s, gys.reshape(ns_tok, sw), gas, gbs, wbr, wv, wg, ns_tok)
        hs_mid, u2s, tis, gts = _outproj(merged_s, hs, mod_s[2], mod_s[4], mod_s[3], n2g, wo, wr_t, br_col,
                                         ns_tok, 1)
        outs["ks"].append(ks.reshape(db, tq, n_heads, dh))
        outs["vs"].append(vs.reshape(db, tq, n_heads, dh))
        outs["srs"].append(srs.reshape(db, n_groups, n_st))
        outs["sis"].append(sis.reshape(db, n_groups, n_st))

        n_pad = -(-n_tok // RANK_TM) * RANK_TM
        top_i = jnp.concatenate([tip, tis, jnp.full((MOE_TOPK, n_pad - n_tok), -1, I32)], axis=1)
        rank, counts = _expert_ranks(top_i, n_exp)
        padded = (counts + MOE_SUB - 1) // MOE_SUB * MOE_SUB
        pad_end = jnp.cumsum(padded)
        pad_start = pad_end - padded
        is_exp = top_i[:, :n_tok, None] == jnp.arange(n_exp, dtype=I32)
        dest = jnp.sum(jnp.where(is_exp, pad_start, 0), axis=-1) + rank[:, :n_tok]
        n_sub_max = -(-(n_tok * MOE_TOPK) // MOE_SUB) + n_exp
        n_rows = n_sub_max * MOE_SUB
        assert n_tok < (1 << (32 - SLOT_BITS)) and n_rows <= (1 << SLOT_BITS)
        tok_ids = jnp.broadcast_to(jnp.arange(n_tok, dtype=jnp.uint32), (MOE_TOPK, n_tok))
        real = (tok_ids << SLOT_BITS) | (tok_ids + jnp.arange(MOE_TOPK, dtype=jnp.uint32)[:, None] * n_tok)
        empty = jnp.uint32(0xFFFFFFFF)
        desc = jnp.full((n_rows,), empty, jnp.uint32).at[dest.reshape(-1)].set(real.reshape(-1))
        is_empty = desc == empty
        spare = (n_tok * MOE_TOPK - 1 + jnp.cumsum(is_empty.astype(I32))).astype(jnp.uint32)
        desc = jnp.where(is_empty, spare, desc)
        packed = lax.bitcast_convert_type(jnp.concatenate([desc, jnp.zeros((MOE_SUB,), jnp.uint32)]), I32)
        used_sub = pad_end[-1:] // MOE_SUB
        sub_start = jnp.concatenate([pad_start // MOE_SUB, used_sub]).astype(I32)
        n_sub = jnp.concatenate([padded // MOE_SUB, n_sub_max - used_sub]).astype(I32)

        u_all = jnp.concatenate([u2p, u2s], axis=0)
        slots = _experts(u_all, sub_start, n_sub, packed, _prep_gate_up(w_gate_up[l]), w_down[l].astype(BF16),
                         _group_bias(b_gate_up[l]), b_down[l][:, None, :], n_rows)

        gates_t = jnp.concatenate([gtp, gts], axis=1).T
        last = l == depth - 1
        fg = final_g[None] if last else None
        assert last, "depth > 1 needs the un-normalised residual stream"
        hp = _combine(slots, n_tok, 0, gates_t[:np_tok], hp_mid, mod_p[5], fg, COMBINE_TM, t // COMBINE_TM)
        hs = _combine(slots, n_tok, np_tok, gates_t[np_tok:], hs_mid, mod_s[5], fg, ns_tok, 1)

    st = lambda name: jnp.stack(outs[name])
    return (hp.reshape(bp, t, d), hs.reshape(db, tq, d),
            st("kp"), st("vp"), st("srp"), st("sip"), st("ks"), st("vs"), st("srs"), st("sis"))
```

```python
import functools
import math

import jax
import jax.numpy as jnp
from jax import lax
from jax.experimental import pallas as pl
from jax.experimental.pallas import tpu as pltpu

F32 = jnp.float32
BF16 = jnp.bfloat16
I32 = jnp.int32
HIGHEST = lax.Precision.HIGHEST

MOBA_BLOCK = 256
MOBA_TOPK = 3
SSM_GROUP = 16
MOE_TOPK = 4
SWIGLU_ALPHA = 1.702
SWIGLU_LIMIT = 7.0
RMS_EPS = 1e-5
N_MOD = 6

LANES = 128
SUBLANES = 8
VMEM_LIMIT_BYTES = 56 * 1024 * 1024

ADA_TN = 1024
PROJ_TN = 1024
PROJ_TM = 512
SCAN_TC = 64
SCAN_SLAB = 256
SCAN_COLS = 512
SCAN_UNROLL = 4
RANK_TM = 512
MOE_SUB = 512
MOE_GROUP = 512
SLOT_BITS = 17
SLOT_MASK = (1 << SLOT_BITS) - 1
MIX_TM = 256
COMBINE_TM = 128
KMEAN_BLOCKS_PER_STEP = 8

NEG = -1e30


def _cparams(sem, vmem=VMEM_LIMIT_BYTES):
    return pltpu.CompilerParams(dimension_semantics=sem, vmem_limit_bytes=vmem)


def _dot(a, b):
    return jnp.dot(a, b, preferred_element_type=F32)


def _dot_nt(a, b, precision=None):
    return lax.dot_general(a, b, (((1,), (1,)), ((), ())), precision=precision,
                           preferred_element_type=F32)


def _sigmoid(x):
    return 1.0 / (1.0 + jnp.exp(-x))


def _rmsnorm(x, g):
    return (x * lax.rsqrt(jnp.mean(x * x, axis=-1, keepdims=True) + RMS_EPS)) * g


def _ada_kernel(c_ref, w_ref, b_ref, o_ref):
    c = c_ref[...]
    s = (c * _sigmoid(c)).astype(BF16)
    o_ref[...] = _dot(s, w_ref[...].astype(BF16)) + b_ref[...]


def _ada(c, w, b):
    r, d = c.shape
    m = w.shape[1]
    return pl.pallas_call(
        _ada_kernel,
        grid=(m // ADA_TN,),
        in_specs=[pl.BlockSpec((r, d), lambda j: (0, 0)),
                  pl.BlockSpec((d, ADA_TN), lambda j: (0, j)),
                  pl.BlockSpec((1, ADA_TN), lambda j: (0, j))],
        out_specs=pl.BlockSpec((r, ADA_TN), lambda j: (0, j)),
        out_shape=jax.ShapeDtypeStruct((r, m), F32),
        compiler_params=_cparams(("parallel",)),
        name="ada_mod",
    )(c, w, b)


def _inproj_kernel(x_ref, sc_ref, sh_ref, g_ref, w_ref,
                   q_ref, k_ref, v_ref, xs_ref, ga_ref, gb_ref, u_scr):
    j = pl.program_id(1)

    @pl.when(j == 0)
    def _():
        u = _rmsnorm(x_ref[...], g_ref[...]) * (1.0 + sc_ref[0]) + sh_ref[0]
        u_scr[...] = u.astype(BF16)

    r = _dot(u_scr[...], w_ref[...])
    for jj, ref in ((0, q_ref), (1, k_ref), (2, v_ref), (3, xs_ref)):
        @pl.when(j == jj)
        def _(ref=ref):
            ref[...] = r

    @pl.when((j == 4) | (j == 5))
    def _():
        ga_ref[...] = r.astype(BF16)

    @pl.when(j >= 6)
    def _():
        gb_ref[...] = r.astype(BF16)


def _inproj(x, sc, sh, g, w_bf, tm, tiles_per_group):
    n, d = x.shape
    tn = PROJ_TN
    assert w_bf.shape[1] == 8 * tn and d == 2 * tn, "projection layout: q|k|v|ssm each one tile, gates two"
    r = sc.shape[1]
    mod_spec = pl.BlockSpec((1, r, d), lambda i, j: (i // tiles_per_group, 0, 0))
    row = lambda i, j: (i, 0)
    outs = (
        jax.ShapeDtypeStruct((n, tn), F32),
        jax.ShapeDtypeStruct((n, tn), F32),
        jax.ShapeDtypeStruct((n, tn), F32),
        jax.ShapeDtypeStruct((n, tn), F32),
        jax.ShapeDtypeStruct((n, 2 * tn), BF16),
        jax.ShapeDtypeStruct((n, 2 * tn), BF16),
    )
    return pl.pallas_call(
        _inproj_kernel,
        grid=(n // tm, 8),
        in_specs=[pl.BlockSpec((tm, d), row), mod_spec, mod_spec,
                  pl.BlockSpec((1, d), lambda i, j: (0, 0)),
                  pl.BlockSpec((d, tn), lambda i, j: (0, j))],
        out_specs=(pl.BlockSpec((tm, tn), row), pl.BlockSpec((tm, tn), row),
                   pl.BlockSpec((tm, tn), row), pl.BlockSpec((tm, tn), row),
                   pl.BlockSpec((tm, tn), lambda i, j: (i, jnp.clip(j - 4, 0, 1))),
                   pl.BlockSpec((tm, tn), lambda i, j: (i, jnp.clip(j - 6, 0, 1)))),
        out_shape=outs,
        scratch_shapes=[pltpu.VMEM((tm, d), BF16)],
        compiler_params=_cparams(("parallel", "arbitrary")),
        name="in_proj",
    )(x, sc, sh, g, w_bf)


def _moba_prompt_kernel(q_ref, k_ref, v_ref, o_ref, kb_scr, vt_scr, km_scr, *, nb, scale):
    qi = pl.program_id(2)
    blk = MOBA_BLOCK
    dh = q_ref.shape[-1]

    @pl.when(qi == 0)
    def _():
        k = k_ref[...]
        km_scr[...] = jnp.sum(k.reshape(nb, blk, dh), axis=1) * (1.0 / blk)
        for j in range(nb):
            kb_scr[j] = k[j * blk:(j + 1) * blk].astype(BF16)
            vt_scr[j] = v_ref[j * blk:(j + 1) * blk, :].T.astype(BF16)

    q = q_ref[...]
    gate_t = _dot_nt(km_scr[...], q, precision=HIGHEST)
    jidx = lax.broadcasted_iota(I32, gate_t.shape, 0)
    cnt = jnp.zeros(gate_t.shape, I32)
    for jp in range(nb):
        row = gate_t[jp:jp + 1, :]
        beats = (row > gate_t) | ((row == gate_t) & (jp < jidx))
        cnt = cnt + jnp.where(jp < qi, jnp.where(beats, 1, 0), 0)
    sel = jnp.where((cnt < MOBA_TOPK) & (jidx < qi), 1.0, 0.0)

    qb = q.astype(BF16)
    causal = (lax.broadcasted_iota(I32, (blk, blk), 0) <= lax.broadcasted_iota(I32, (blk, blk), 1))

    def attend(own):
        blocks = []
        m = None
        for j in range(own + 1):
            s = _dot_nt(kb_scr[j], qb) * scale
            s = jnp.where(causal if j == own else sel[j:j + 1, :] > 0.5, s, NEG)
            blocks.append(s)
            mj = jnp.max(s, axis=0, keepdims=True)
            m = mj if m is None else jnp.maximum(m, mj)
        l = jnp.zeros((1, blk), F32)
        acc = jnp.zeros((dh, blk), F32)
        for j in range(own + 1):
            p = jnp.exp(blocks[j] - m)
            l = l + jnp.sum(p, axis=0, keepdims=True)
            acc = acc + _dot(vt_scr[j], p.astype(BF16))
        o_ref[...] = (acc / l).T.astype(o_ref.dtype)

    for own in range(nb):
        @pl.when(qi == own)
        def _(own=own):
            attend(own)


def _moba_prompt(q, k, v, n_heads):
    b, t, w = q.shape
    dh = w // n_heads
    assert t % MOBA_BLOCK == 0 and dh == LANES
    nb = t // MOBA_BLOCK
    kern = functools.partial(_moba_prompt_kernel, nb=nb, scale=dh ** -0.5)
    kv_spec = pl.BlockSpec((None, t, dh), lambda bi, h, qi: (bi, 0, h))
    q_spec = pl.BlockSpec((None, MOBA_BLOCK, dh), lambda bi, h, qi: (bi, qi, h))
    return pl.pallas_call(
        kern,
        grid=(b, n_heads, nb),
        in_specs=[q_spec, kv_spec, kv_spec],
        out_specs=q_spec,
        out_shape=jax.ShapeDtypeStruct((b, t, w), BF16),
        scratch_shapes=[pltpu.VMEM((nb, MOBA_BLOCK, dh), BF16), pltpu.VMEM((nb, dh, MOBA_BLOCK), BF16),
                        pltpu.VMEM((nb, dh), F32)],
        compiler_params=_cparams(("parallel", "parallel", "arbitrary")),
        name="moba_prompt",
    )(q, k, v)


def _kmean_kernel(pt_ref, *refs, ppb, bps):
    del pt_ref
    o_ref = refs[ppb * bps]
    j = pl.program_id(1)
    for n in range(bps):
        s = jnp.sum(refs[n * ppb][...], axis=0)
        for p in range(1, ppb):
            s = s + jnp.sum(refs[n * ppb + p][...], axis=0)
        o_ref[pl.ds(j * bps + n, 1)] = (s * (1.0 / MOBA_BLOCK))[None]


def _cache_block_means(cache_k, layer, page_table, n_blocks):
    _, _, page, h, dh = cache_k.shape
    db = page_table.shape[0]
    ppb = MOBA_BLOCK // page
    bps = math.gcd(n_blocks, KMEAN_BLOCKS_PER_STEP)
    page_specs = [
        pl.BlockSpec((None, None, page, h, dh),
                     functools.partial(lambda b, j, pt, p: (layer, pt[b, j * ppb * bps + p], 0, 0, 0), p=p))
        for p in range(ppb * bps)]
    return pl.pallas_call(
        functools.partial(_kmean_kernel, ppb=ppb, bps=bps),
        grid_spec=pltpu.PrefetchScalarGridSpec(
            num_scalar_prefetch=1,
            grid=(db, n_blocks // bps),
            in_specs=page_specs,
            out_specs=pl.BlockSpec((None, n_blocks, h, dh), lambda b, j, pt: (b, 0, 0, 0))),
        out_shape=jax.ShapeDtypeStruct((db, n_blocks, h, dh), F32),
        compiler_params=_cparams(("parallel", "arbitrary")),
        name="cache_block_means",
    )(page_table, *([cache_k] * (ppb * bps)))


def _select_kernel(q_ref, km_ref, o_ref, *, n_heads, n_blocks):
    dh = km_ref.shape[-1]
    rows = []
    for h in range(n_heads):
        qh = q_ref[:, h * dh:(h + 1) * dh]
        rows.append(_dot_nt(qh, km_ref[:, h, :], precision=HIGHEST))
    g = jnp.concatenate(rows, axis=0)
    lane = lax.broadcasted_iota(I32, g.shape, 1)
    out_lane = lax.broadcasted_iota(I32, o_ref.shape, 1)
    out = jnp.zeros(o_ref.shape, I32)
    for t in range(MOBA_TOPK):
        m = jnp.max(g, axis=1, keepdims=True)
        idx = jnp.min(jnp.where(g == m, lane, n_blocks), axis=1, keepdims=True)
        out = jnp.where(out_lane == t, idx, out)
        g = jnp.where(lane == idx, -jnp.inf, g)
    o_ref[...] = out


def _select_blocks(q_pad, kmean, n_heads):
    db, qp, w = q_pad.shape
    n_blocks = kmean.shape[1]
    dh = w // n_heads
    return pl.pallas_call(
        functools.partial(_select_kernel, n_heads=n_heads, n_blocks=n_blocks),
        grid=(db,),
        in_specs=[pl.BlockSpec((None, qp, w), lambda b: (b, 0, 0)),
                  pl.BlockSpec((None, n_blocks, n_heads, dh), lambda b: (b, 0, 0, 0))],
        out_specs=pl.BlockSpec((None, n_heads * qp, LANES), lambda b: (b, 0, 0)),
        out_shape=jax.ShapeDtypeStruct((db, n_heads * qp, LANES), I32),
        compiler_params=_cparams(("parallel",)),
        name="moba_select",
    )(q_pad, kmean)


def _sample_attn_kernel(sel_ref, pt_ref, q_ref, kn_ref, vn_ref, ck_hbm, cv_hbm, o_ref,
                        kbuf, vbuf, sems, *, layer, n_heads, n_q, qp, page, scale):
    b = pl.program_id(0)
    ppb = MOBA_BLOCK // page
    dh = kbuf.shape[-1]
    per_head = n_q * MOBA_TOPK * ppb
    span = MOBA_TOPK * MOBA_BLOCK

    def copies(i):
        h = i // per_head
        rem = i % per_head
        qs = rem // ppb
        p = rem % ppb
        q = qs // MOBA_TOPK
        s = qs % MOBA_TOPK
        blk = sel_ref[b, (h * qp + q) * MOBA_TOPK + s]
        pg = pt_ref[b, blk * ppb + p]
        dst = pl.ds(qs * MOBA_BLOCK + p * page, page)
        return (pltpu.make_async_copy(ck_hbm.at[layer, pg, :, h, :], kbuf.at[h, dst, :], sems.at[0]),
                pltpu.make_async_copy(cv_hbm.at[layer, pg, :, h, :], vbuf.at[h, dst, :], sems.at[1]))

    def start(i, c):
        ck, cv = copies(i)
        ck.start()
        cv.start()
        return c

    def wait(i, c):
        ck, cv = copies(i)
        ck.wait()
        cv.wait()
        return c

    lax.fori_loop(0, n_heads * per_head, start, 0)
    lax.fori_loop(0, n_heads * per_head, wait, 0)

    n_keys = n_q * span
    row = lax.broadcasted_iota(I32, (qp, n_keys), 0)
    col = lax.broadcasted_iota(I32, (qp, n_keys), 1)
    own = (col >= row * span) & (col < (row + 1) * span)
    rown = lax.broadcasted_iota(I32, (qp, qp), 0)
    coln = lax.broadcasted_iota(I32, (qp, qp), 1)
    causal = (coln <= rown) & (coln < n_q)
    for h in range(n_heads):
        hs = slice(h * dh, (h + 1) * dh)
        qh = q_ref[:, hs].astype(BF16)
        s = jnp.where(own, _dot_nt(qh, kbuf[h].astype(BF16)) * scale, NEG)
        sn = jnp.where(causal, _dot_nt(qh, kn_ref[:, hs].astype(BF16)) * scale, NEG)
        m = jnp.maximum(jnp.max(s, axis=1, keepdims=True), jnp.max(sn, axis=1, keepdims=True))
        p = jnp.exp(s - m)
        pn = jnp.exp(sn - m)
        l = jnp.sum(p, axis=1, keepdims=True) + jnp.sum(pn, axis=1, keepdims=True)
        o = _dot(p.astype(BF16), vbuf[h].astype(BF16)) + _dot(pn.astype(BF16), vn_ref[:, hs].astype(BF16))
        o_ref[:, hs] = o / l


def _sample_attention(q_pad, k_pad, v_pad, sel_flat, page_table, cache_k, cache_v, layer, n_heads, n_q):
    db, qp, w = q_pad.shape
    dh = w // n_heads
    page = cache_k.shape[2]
    n_keys = n_q * MOBA_TOPK * MOBA_BLOCK
    kern = functools.partial(_sample_attn_kernel, layer=layer, n_heads=n_heads, n_q=n_q, qp=qp,
                             page=page, scale=dh ** -0.5)
    row_spec = pl.BlockSpec((None, qp, w), lambda b, sel, pt: (b, 0, 0))
    return pl.pallas_call(
        kern,
        grid_spec=pltpu.PrefetchScalarGridSpec(
            num_scalar_prefetch=2,
            grid=(db,),
            in_specs=[row_spec, row_spec, row_spec,
                      pl.BlockSpec(memory_space=pl.ANY), pl.BlockSpec(memory_space=pl.ANY)],
            out_specs=row_spec,
            scratch_shapes=[pltpu.VMEM((n_heads, n_keys, dh), F32),
                            pltpu.VMEM((n_heads, n_keys, dh), F32),
                            pltpu.SemaphoreType.DMA((2,))]),
        out_shape=jax.ShapeDtypeStruct((db, qp, w), F32),
        compiler_params=_cparams(("arbitrary",)),
        name="moba_sample",
    )(sel_flat, page_table, q_pad, k_pad, v_pad, cache_k, cache_v)


def _s5_param_kernel(are_ref, aim_ref, ldt_ref, bre_ref, bim_ref,
                     o_are, o_aim, o_bre, o_bim):
    a_re = are_ref[...]
    a_im = aim_ref[...]
    dt = jnp.exp(ldt_ref[...])
    mag = jnp.exp(a_re * dt)
    ab_re = mag * jnp.cos(a_im * dt)
    ab_im = mag * jnp.sin(a_im * dt)
    den = a_re * a_re + a_im * a_im
    nr = ab_re - 1.0
    ni = ab_im
    c_re = (nr * a_re + ni * a_im) / den
    c_im = (ni * a_re - nr * a_im) / den
    b_re = bre_ref[...]
    b_im = bim_ref[...]
    o_are[...] = ab_re
    o_aim[...] = ab_im
    o_bre[...] = c_re * b_re - c_im * b_im
    o_bim[...] = c_re * b_im + c_im * b_re


def _s5_params(a_re, a_im, log_dt, b_re, b_im):
    g, p = a_re.shape
    c = b_re.shape[-1]
    bt_re = jnp.swapaxes(b_re, 1, 2)
    bt_im = jnp.swapaxes(b_im, 1, 2)
    sds = jax.ShapeDtypeStruct
    ab_re, ab_im, bb_re, bb_im = pl.pallas_call(
        _s5_param_kernel,
        out_shape=(sds((g, 1, p), F32), sds((g, 1, p), F32), sds((g, c, p), F32), sds((g, c, p), F32)),
        name="s5_params",
    )(a_re.reshape(g, 1, p), a_im.reshape(g, 1, p), log_dt.reshape(g, 1, 1), bt_re, bt_im)
    return ab_re.reshape(g, p), ab_im.reshape(g, p), bb_re, bb_im


def _block_diag_slabs(m, groups_per_slab):
    g, r, c = m.shape
    n = g // groups_per_slab
    eye = jnp.eye(groups_per_slab, dtype=m.dtype)
    m4 = m.reshape(n, groups_per_slab, r, c)
    out = jnp.einsum("ngrc,gh->ngrhc", m4, eye)
    return out.reshape(n, groups_per_slab * r, groups_per_slab * c)


def _s5_kernel(xs_ref, s0re_ref, s0im_ref, are_ref, aim_ref, bre_ref, bim_ref, cre_ref, cim_ref, d_ref,
               gy_ref, fre_ref, fim_ref, u_t, y_t, sre, sim, st_re, st_im):
    c = pl.program_id(0)
    n_batch, rpb, w = xs_ref.shape
    n_slab, slab_in, slab_st = bre_ref.shape
    n_state = sre.shape[1]
    in_tiles = w // LANES
    tiles_per_slab = slab_in // LANES

    @pl.when(c == 0)
    def _():
        st_re[...] = s0re_ref[...]
        st_im[...] = s0im_ref[...]

    for b in range(n_batch):
        for ct in range(in_tiles):
            u_t[ct, pl.ds(b, rpb, stride=n_batch), :] = xs_ref[b, :, ct * LANES:(ct + 1) * LANES]

    for k in range(n_slab):
        u = jnp.concatenate([u_t[k * tiles_per_slab + i] for i in range(tiles_per_slab)], axis=1).astype(BF16)
        sre[:, k * slab_st:(k + 1) * slab_st] = _dot(u, bre_ref[k])
        sim[:, k * slab_st:(k + 1) * slab_st] = _dot(u, bim_ref[k])

    for bg in range(n_batch // SUBLANES):
        brows = slice(bg * SUBLANES, (bg + 1) * SUBLANES)
        for cs in range(n_state // SCAN_COLS):
            cols = slice(cs * SCAN_COLS, (cs + 1) * SCAN_COLS)
            a_re = jnp.broadcast_to(are_ref[:, cols], (SUBLANES, SCAN_COLS))
            a_im = jnp.broadcast_to(aim_ref[:, cols], (SUBLANES, SCAN_COLS))

            def step(t, carry, cols=cols, bg=bg, a_re=a_re, a_im=a_im):
                s_re, s_im = carry
                rows = pl.ds(pl.multiple_of(t * n_batch + bg * SUBLANES, SUBLANES), SUBLANES)
                n_re = a_re * s_re - a_im * s_im + sre[rows, cols]
                n_im = a_re * s_im + a_im * s_re + sim[rows, cols]
                sre[rows, cols] = n_re
                sim[rows, cols] = n_im
                return n_re, n_im

            f_re, f_im = lax.fori_loop(0, rpb, step, (st_re[brows, cols], st_im[brows, cols]),
                                       unroll=SCAN_UNROLL)
            st_re[brows, cols] = f_re
            st_im[brows, cols] = f_im

    fre_ref[...] = st_re[...]
    fim_ref[...] = st_im[...]

    for k in range(n_slab):
        scols = slice(k * slab_st, (k + 1) * slab_st)
        y = _dot(sre[:, scols].astype(BF16), cre_ref[k]) - _dot(sim[:, scols].astype(BF16), cim_ref[k])
        for i in range(tiles_per_slab):
            ct = k * tiles_per_slab + i
            yy = y[:, i * LANES:(i + 1) * LANES] + d_ref[:, ct * LANES:(ct + 1) * LANES] * u_t[ct]
            inner = math.sqrt(2.0 / math.pi) * (yy + 0.044715 * (yy * yy * yy))
            y_t[ct] = 0.5 * yy * (1.0 + jnp.tanh(inner))

    for b in range(n_batch):
        for ct in range(in_tiles):
            gy_ref[b, :, ct * LANES:(ct + 1) * LANES] = y_t[ct, pl.ds(b, rpb, stride=n_batch), :]


def _s5(xs3, s0_re, s0_im, abar_re, abar_im, bre_bd, bim_bd, cre_bd, cim_bd, d_row, rows_per_batch):
    n_batch, t, w = xs3.shape
    rpb = rows_per_batch
    n_state = abar_re.shape[1]
    rows = n_batch * rpb
    assert t % rpb == 0 and n_batch % SUBLANES == 0 and n_state % SCAN_COLS == 0 and w % LANES == 0
    full = lambda a: pl.BlockSpec(a.shape, lambda c: (0,) * a.ndim)
    st_spec = pl.BlockSpec((n_batch, n_state), lambda c: (0, 0))
    x_spec = pl.BlockSpec((n_batch, rpb, w), lambda c: (0, c, 0))
    sds = jax.ShapeDtypeStruct
    return pl.pallas_call(
        _s5_kernel,
        grid=(t // rpb,),
        in_specs=[x_spec, st_spec, st_spec, full(abar_re), full(abar_im), full(bre_bd), full(bim_bd),
                  full(cre_bd), full(cim_bd), full(d_row)],
        out_specs=(x_spec, st_spec, st_spec),
        out_shape=(sds((n_batch, t, w), F32), sds((n_batch, n_state), F32), sds((n_batch, n_state), F32)),
        scratch_shapes=[pltpu.VMEM((w // LANES, rows, LANES), F32), pltpu.VMEM((w // LANES, rows, LANES), F32),
                        pltpu.VMEM((rows, n_state), F32), pltpu.VMEM((rows, n_state), F32),
                        pltpu.VMEM((n_batch, n_state), F32), pltpu.VMEM((n_batch, n_state), F32)],
        compiler_params=_cparams(("arbitrary",)),
        name="s5_scan",
    )(xs3, s0_re, s0_im, abar_re, abar_im, bre_bd, bim_bd, cre_bd, cim_bd, d_row)


def _merge_kernel(att_ref, gy_ref, ga_ref, gb_ref, wbr_ref, wv_ref, wg_ref, o_ref):
    y_att = _dot(att_ref[...], wbr_ref[...])
    gy = gy_ref[...].astype(BF16)
    y_ssm = _dot(gy, wv_ref[...]) * _sigmoid(_dot(gy, wg_ref[...]))
    merged = _sigmoid(ga_ref[...].astype(F32)) * y_att + _sigmoid(gb_ref[...].astype(F32)) * y_ssm
    o_ref[...] = merged.astype(o_ref.dtype)


def _merge(att, gy, ga, gb, wbr, wv, wg, tm):
    n, aw = att.shape
    sw = gy.shape[1]
    d = ga.shape[1]
    row = lambda i: (i, 0)
    const = lambda i: (0, 0)
    return pl.pallas_call(
        _merge_kernel,
        grid=(n // tm,),
        in_specs=[pl.BlockSpec((tm, aw), row), pl.BlockSpec((tm, sw), row),
                  pl.BlockSpec((tm, d), row), pl.BlockSpec((tm, d), row),
                  pl.BlockSpec((aw, d), const), pl.BlockSpec((sw, d), const), pl.BlockSpec((sw, d), const)],
        out_specs=pl.BlockSpec((tm, d), row),
        out_shape=jax.ShapeDtypeStruct((n, d), BF16),
        compiler_params=_cparams(("parallel",)),
        name="branch_merge",
    )(att, gy, ga, gb, wbr, wv, wg)


def _outproj_kernel(m_ref, x_ref, g1_ref, sc_ref, sh_ref, ng_ref, wo_ref, wr_ref, br_ref,
                    h_ref, u_ref, ti_ref, gt_ref):
    h = x_ref[...] + g1_ref[0] * _dot(m_ref[...], wo_ref[...])
    h_ref[...] = h
    u = _rmsnorm(h, ng_ref[...]) * (1.0 + sc_ref[0]) + sh_ref[0]
    u_ref[...] = u
    logits = _dot_nt(wr_ref[...], u, precision=HIGHEST) + br_ref[...]
    n_exp = logits.shape[0]
    eidx = lax.broadcasted_iota(I32, logits.shape, 0)
    vals, idxs = [], []
    for _ in range(MOE_TOPK):
        m = jnp.max(logits, axis=0, keepdims=True)
        idx = jnp.min(jnp.where(logits == m, eidx, n_exp), axis=0, keepdims=True)
        vals.append(m)
        idxs.append(idx)
        logits = jnp.where(eidx == idx, -jnp.inf, logits)
    es = [jnp.exp(v - vals[0]) for v in vals]
    tot = es[0]
    for e in es[1:]:
        tot = tot + e
    ti_ref[...] = jnp.concatenate(idxs, axis=0)
    gt_ref[...] = jnp.concatenate([e / tot for e in es], axis=0)


def _outproj(merged, x, g1, sc, sh, ng, wo_bf, wr_t, br_col, tm, tiles_per_group):
    n, d = x.shape
    n_exp = wr_t.shape[0]
    r = sc.shape[1]
    mod_spec = pl.BlockSpec((1, r, d), lambda i: (i // tiles_per_group, 0, 0))
    row = lambda i: (i, 0)
    const = lambda i: (0, 0)
    sds = jax.ShapeDtypeStruct
    return pl.pallas_call(
        _outproj_kernel,
        grid=(n // tm,),
        in_specs=[pl.BlockSpec((tm, d), row), pl.BlockSpec((tm, d), row), mod_spec, mod_spec, mod_spec,
                  pl.BlockSpec((1, d), const), pl.BlockSpec((d, d), const),
                  pl.BlockSpec((n_exp, d), const), pl.BlockSpec((n_exp, 1), const)],
        out_specs=(pl.BlockSpec((tm, d), row), pl.BlockSpec((tm, d), row),
                   pl.BlockSpec((MOE_TOPK, tm), lambda i: (0, i)),
                   pl.BlockSpec((MOE_TOPK, tm), lambda i: (0, i))),
        out_shape=(sds((n, d), F32), sds((n, d), F32), sds((MOE_TOPK, n), I32), sds((MOE_TOPK, n), F32)),
        compiler_params=_cparams(("parallel",)),
        name="out_proj_router",
    )(merged, x, g1, sc, sh, ng, wo_bf, wr_t, br_col)


def _rank_kernel(ti_ref, rank_ref, cnt_ref, carry, *, n_exp):
    i = pl.program_id(0)
    tm = ti_ref.shape[1]

    @pl.when(i == 0)
    def _():
        carry[...] = jnp.zeros(carry.shape, F32)

    ti = ti_ref[...]
    eidx = lax.broadcasted_iota(I32, (n_exp, tm), 0)
    tri = jnp.where(lax.broadcasted_iota(I32, (tm, tm), 0) <= lax.broadcasted_iota(I32, (tm, tm), 1),
                    1.0, 0.0).astype(BF16)
    base = carry[...]
    ranks = []
    for k in range(MOE_TOPK):
        hit = eidx == ti[k:k + 1, :]
        onehot = jnp.where(hit, 1.0, 0.0)
        incl = _dot(onehot.astype(BF16), tri)
        ranks.append(jnp.sum(jnp.where(hit, base + incl - 1.0, 0.0), axis=0, keepdims=True))
        base = base + jnp.sum(onehot, axis=1, keepdims=True)
    carry[...] = base
    rank_ref[...] = jnp.concatenate(ranks, axis=0).astype(I32)
    cnt_ref[...] = jnp.broadcast_to(base, cnt_ref.shape)


def _expert_ranks(top_i, n_exp):
    k, n = top_i.shape
    rank, cnt = pl.pallas_call(
        functools.partial(_rank_kernel, n_exp=n_exp),
        grid=(n // RANK_TM,),
        in_specs=[pl.BlockSpec((k, RANK_TM), lambda i: (0, i))],
        out_specs=(pl.BlockSpec((k, RANK_TM), lambda i: (0, i)),
                   pl.BlockSpec((n_exp, LANES), lambda i: (0, 0))),
        out_shape=(jax.ShapeDtypeStruct((k, n), I32), jax.ShapeDtypeStruct((n_exp, LANES), F32)),
        scratch_shapes=[pltpu.VMEM((n_exp, 1), F32)],
        compiler_params=_cparams(("arbitrary",)),
        name="expert_ranks",
    )(top_i)
    return rank, cnt[:, 0].astype(I32)


def _wprep_kernel(w_ref, perm_ref, o_ref):
    o_ref[...] = _dot(w_ref[...].astype(BF16), perm_ref[...]).astype(BF16)


def _prep_gate_up(w_gu):
    n_exp, d, ff2 = w_gu.shape
    g = MOE_GROUP
    col = jnp.arange(g, dtype=I32)
    target = jnp.where(col % 2 == 0, col // 2, g // 2 + col // 2)
    perm = (target[:, None] == col[None, :]).astype(BF16)
    return pl.pallas_call(
        _wprep_kernel,
        grid=(n_exp, ff2 // g),
        in_specs=[pl.BlockSpec((None, d, g), lambda e, j: (e, 0, j)),
                  pl.BlockSpec((g, g), lambda e, j: (0, 0))],
        out_specs=pl.BlockSpec((None, d, g), lambda e, j: (e, 0, j)),
        out_shape=jax.ShapeDtypeStruct((n_exp, d, ff2), BF16),
        compiler_params=_cparams(("parallel", "parallel")),
        name="moe_weight_prep",
    )(w_gu, perm)


def _group_bias(b_gu):
    n_exp, ff2 = b_gu.shape
    b = b_gu.reshape(n_exp, ff2 // MOE_GROUP, MOE_GROUP // 2, 2)
    return jnp.swapaxes(b, 2, 3).reshape(n_exp, 1, ff2)


def _expert_kernel(ss_ref, ns_ref, pk_ref, bgu_ref, bd_ref, u_hbm, wgu_hbm, wd_hbm, slots_hbm,
                   wgu_v, wd_v, wd_st, x_st, xb_scr, act_scr, o_st, sem_w, sem_g, sem_s, *, n_exp):
    e = pl.program_id(0)
    sub0 = ss_ref[e]
    nsub = ns_ref[e]
    sub = MOE_SUB
    n_grp = wgu_v.shape[1] // MOE_GROUP
    half = MOE_GROUP // 2
    chunk = sub // n_grp

    def gather(row, slot, r):
        tok = lax.shift_right_logical(pk_ref[row], jnp.int32(SLOT_BITS))
        return pltpu.make_async_copy(u_hbm.at[pl.ds(tok, 1), :], x_st.at[slot, pl.ds(r, 1), :], sem_g.at[slot])

    def scatter(row, slot, r):
        dst = pk_ref[row] & SLOT_MASK
        return pltpu.make_async_copy(o_st.at[slot, pl.ds(r, 1), :], slots_hbm.at[pl.ds(dst, 1), :],
                                     sem_s.at[slot])

    def wait_rows(make, slot):
        def body(r, c):
            make(0, slot, 0).wait()
            return c
        lax.fori_loop(0, sub, body, 0, unroll=8)

    def start_rows(make, base, slot):
        def body(r, c):
            make(base + r, slot, r).start()
            return c
        lax.fori_loop(0, sub, body, 0, unroll=8)

    d_rows = wd_v.shape[0] // n_grp

    def w_copies():
        gu = [pltpu.make_async_copy(wgu_hbm.at[e, :, pl.ds(g * MOE_GROUP, MOE_GROUP)],
                                    wgu_v.at[:, pl.ds(g * MOE_GROUP, MOE_GROUP)], sem_w.at[g])
              for g in range(n_grp)]
        dn = [pltpu.make_async_copy(wd_hbm.at[e, pl.ds(c * d_rows, d_rows), :], wd_st.at[c % 2],
                                    sem_w.at[n_grp + c % 2])
              for c in range(n_grp)]
        return gu, dn

    def block(gsub, first):
        slot = gsub % 2
        other = 1 - slot
        prev = jnp.where(gsub == 0, used_sub, gsub - 1)
        gu_copies, d_copies = w_copies()
        wait_rows(gather, slot)
        xb_scr[...] = x_st[slot].astype(BF16)
        for g in range(n_grp):
            cols = slice(g * MOE_GROUP, (g + 1) * MOE_GROUP)
            if first:
                gu_copies[g].wait()
            h = _dot(xb_scr[...], wgu_v[:, cols]) + bgu_ref[:, cols]
            h_glu = jnp.minimum(h[:, :half], SWIGLU_LIMIT)
            h_lin = jnp.clip(h[:, half:], -SWIGLU_LIMIT, SWIGLU_LIMIT)
            act = h_glu * _sigmoid(SWIGLU_ALPHA * h_glu) * (h_lin + 1.0)
            act_scr[:, g * half:(g + 1) * half] = act.astype(BF16)
            for r in range(g * chunk, (g + 1) * chunk):
                gather((gsub + 1) * sub + r, other, r).start()
                scatter(prev * sub + r, other, r).start()
            if first:
                d_copies[g].wait()
                wd_v[g * d_rows:(g + 1) * d_rows, :] = wd_st[g % 2].astype(BF16)
                if g + 2 < n_grp:
                    d_copies[g + 2].start()
        o_st[slot] = _dot(act_scr[...], wd_v[...]) + bd_ref[...]
        wait_rows(scatter, other)

    used_sub = ss_ref[n_exp]

    @pl.when(e == 0)
    def _():
        o_st[1] = jnp.zeros(o_st.shape[1:], o_st.dtype)
        start_rows(gather, 0, 0)

    @pl.when(nsub > 0)
    def _():
        gu_copies, d_copies = w_copies()
        for c in gu_copies:
            c.start()
        d_copies[0].start()
        d_copies[1].start()
        block(sub0, True)

        def body(s, c):
            block(sub0 + s, False)
            return c

        lax.fori_loop(1, nsub, body, 0)

    @pl.when(e == n_exp - 1)
    def _():
        last = used_sub - 1
        lslot = last % 2
        wait_rows(gather, 1 - lslot)
        start_rows(scatter, last * sub, lslot)
        wait_rows(scatter, lslot)
        o_st[0] = jnp.zeros(o_st.shape[1:], o_st.dtype)

        def tail(i, c):
            start_rows(scatter, (used_sub + 1 + i) * sub, 0)
            wait_rows(scatter, 0)
            return c

        lax.fori_loop(0, ns_ref[n_exp] - 1, tail, 0)


def _experts(u_all, sub_start, n_sub, packed, w_gu_bf, w_d, b_gu, b_d, n_rows):
    n, d = u_all.shape
    n_exp, _, ff2 = w_gu_bf.shape
    ff = ff2 // 2
    sub = MOE_SUB
    n_grp = ff2 // MOE_GROUP
    assert ff2 % MOE_GROUP == 0 and sub % n_grp == 0 and ff % (16 * n_grp) == 0 and n_grp >= 2
    return pl.pallas_call(
        functools.partial(_expert_kernel, n_exp=n_exp),
        grid_spec=pltpu.PrefetchScalarGridSpec(
            num_scalar_prefetch=3,
            grid=(n_exp,),
            in_specs=[pl.BlockSpec((None, 1, ff2), lambda e, ss, ns, pk: (e, 0, 0)),
                      pl.BlockSpec((None, 1, d), lambda e, ss, ns, pk: (e, 0, 0)),
                      pl.BlockSpec(memory_space=pl.ANY), pl.BlockSpec(memory_space=pl.ANY),
                      pl.BlockSpec(memory_space=pl.ANY)],
            out_specs=pl.BlockSpec(memory_space=pl.ANY),
            scratch_shapes=[pltpu.VMEM((d, ff2), BF16), pltpu.VMEM((ff, d), BF16),
                            pltpu.VMEM((2, ff // n_grp, d), F32),
                            pltpu.VMEM((2, sub, d), F32), pltpu.VMEM((sub, d), BF16),
                            pltpu.VMEM((sub, ff), BF16), pltpu.VMEM((2, sub, d), F32),
                            pltpu.SemaphoreType.DMA((n_grp + 2,)), pltpu.SemaphoreType.DMA((2,)),
                            pltpu.SemaphoreType.DMA((2,))]),
        out_shape=jax.ShapeDtypeStruct((n_rows, d), F32),
        compiler_params=_cparams(("arbitrary",)),
        name="moe_experts",
    )(sub_start, n_sub, packed, b_gu, b_d, u_all, w_gu_bf, w_d)


def _combine_kernel(*refs):
    slot_refs = refs[:MOE_TOPK]
    gt_ref, h_ref, g2_ref, fg_ref, y_ref = refs[MOE_TOPK:]
    gt = gt_ref[...]
    y = gt[:, 0:1] * slot_refs[0][...]
    for k in range(1, MOE_TOPK):
        y = y + gt[:, k:k + 1] * slot_refs[k][...]
    h = h_ref[...] + g2_ref[0] * y
    y_ref[...] = _rmsnorm(h, fg_ref[...])


def _combine(slots, n_all, tok0, gates_t, h, g2, final_g, tm, tiles_per_group):
    n, d = h.shape
    r = g2.shape[1]
    assert n_all % tm == 0 and tok0 % tm == 0
    row = lambda i: (i, 0)
    slot_specs = [pl.BlockSpec((tm, d), functools.partial(lambda i, k: ((k * n_all + tok0) // tm + i, 0), k=k))
                  for k in range(MOE_TOPK)]
    return pl.pallas_call(
        _combine_kernel,
        grid=(n // tm,),
        in_specs=slot_specs + [pl.BlockSpec((tm, MOE_TOPK), row), pl.BlockSpec((tm, d), row),
                               pl.BlockSpec((1, r, d), lambda i: (i // tiles_per_group, 0, 0)),
                               pl.BlockSpec((1, d), lambda i: (0, 0))],
        out_specs=pl.BlockSpec((tm, d), row),
        out_shape=jax.ShapeDtypeStruct((n, d), F32),
        compiler_params=_cparams(("parallel",)),
        name="moe_combine",
    )(*([slots] * MOE_TOPK), gates_t, h, g2, final_g)


def _mods(mod, rows_per_batch, per_row):
    b = mod.shape[0]
    parts = jnp.split(mod, N_MOD, axis=-1)
    if per_row:
        return [jnp.repeat(p, rows_per_batch, axis=0)[None] for p in parts]
    return [p.reshape(b, 1, -1) for p in parts]


def kernel(x_prompt, x_sample, cache_k, cache_v, state_ssm_re, state_ssm_im, page_table, c_prompt, c_sample,
           w_ada, b_ada, norm1_g, w_in, w_attn_br, ssm_a_re, ssm_a_im, ssm_log_dt, ssm_b_re, ssm_b_im,
           ssm_c_re, ssm_c_im, ssm_d, w_ssm_val, w_ssm_gate, w_out, norm2_g, w_router, b_router,
           w_gate_up, b_gate_up, w_down, b_down, final_g):
    depth = w_in.shape[0]
    bp, t, d = x_prompt.shape
    db, tq, _ = x_sample.shape
    _, n_pool, page, n_heads, dh = cache_k.shape
    aw = n_heads * dh
    n_groups, n_st = ssm_a_re.shape[1:]
    sw = n_groups * SSM_GROUP
    n_state = n_groups * n_st
    n_exp = w_router.shape[-1]
    past = page_table.shape[1] * page
    n_past_blocks = past // MOBA_BLOCK
    assert past % MOBA_BLOCK == 0 and MOBA_BLOCK % page == 0, "cached keys must fill whole MoBA blocks"
    assert n_past_blocks >= MOBA_TOPK and tq <= SUBLANES, "sample step: all selected blocks are past blocks"
    assert sw % SCAN_SLAB == 0 and t % SCAN_TC == 0 and t % PROJ_TM == 0

    np_tok = bp * t
    ns_tok = db * tq
    n_tok = np_tok + ns_tok
    qp = SUBLANES

    hp = x_prompt.reshape(np_tok, d)
    hs = x_sample.reshape(ns_tok, d)
    c_all = jnp.concatenate([c_prompt, c_sample], axis=0)

    outs = {k: [] for k in ("kp", "vp", "srp", "sip", "ks", "vs", "srs", "sis")}
    gps = SCAN_SLAB // SSM_GROUP
    for l in range(depth):
        mod = _ada(c_all, w_ada[l], b_ada[l][None])
        mod_p = _mods(mod[:bp], t, per_row=False)
        mod_s = _mods(mod[bp:], tq, per_row=True)

        w_in_bf = w_in[l].astype(BF16)
        wbr = w_attn_br[l].astype(BF16)
        wv = w_ssm_val[l].astype(BF16)
        wg = w_ssm_gate[l].astype(BF16)
        wo = w_out[l].astype(BF16)
        wr_t = w_router[l].T
        br_col = b_router[l][:, None]
        n1g = norm1_g[l][None]
        n2g = norm2_g[l][None]

        abar_re, abar_im, bbt_re, bbt_im = _s5_params(ssm_a_re[l], ssm_a_im[l], ssm_log_dt[l],
                                                      ssm_b_re[l], ssm_b_im[l])
        bre_bd = _block_diag_slabs(bbt_re, gps).astype(BF16)
        bim_bd = _block_diag_slabs(bbt_im, gps).astype(BF16)
        cre_bd = _block_diag_slabs(jnp.swapaxes(ssm_c_re[l], 1, 2), gps).astype(BF16)
        cim_bd = _block_diag_slabs(jnp.swapaxes(ssm_c_im[l], 1, 2), gps).astype(BF16)
        abar_re_row = abar_re.reshape(1, n_state)
        abar_im_row = abar_im.reshape(1, n_state)
        d_row = ssm_d[l].reshape(1, sw)

        tpg = t // PROJ_TM
        q, k, v, xs, ga, gb = _inproj(hp, mod_p[1], mod_p[0], n1g, w_in_bf, PROJ_TM, tpg)
        att = _moba_prompt(q.reshape(bp, t, aw), k.reshape(bp, t, aw), v.reshape(bp, t, aw), n_heads)
        zeros_state = jnp.zeros((bp, n_state), F32)
        gy, srp, sip = _s5(xs.reshape(bp, t, sw), zeros_state, zeros_state, abar_re_row, abar_im_row,
                           bre_bd, bim_bd, cre_bd, cim_bd, d_row, SCAN_TC)
        merged = _merge(att.reshape(np_tok, aw), gy.reshape(np_tok, sw), ga, gb, wbr, wv, wg, MIX_TM)
        hp_mid, u2p, tip, gtp = _outproj(merged, hp, mod_p[2], mod_p[4], mod_p[3], n2g, wo, wr_t, br_col,
                                         MIX_TM, t // MIX_TM)
        outs["kp"].append(k.reshape(bp, t, n_heads, dh))
        outs["vp"].append(v.reshape(bp, t, n_heads, dh))
        outs["srp"].append(srp.reshape(bp, n_groups, n_st))
        outs["sip"].append(sip.reshape(bp, n_groups, n_st))

        qs, ks, vs, xss, gas, gbs = _inproj(hs, mod_s[1], mod_s[0], n1g, w_in_bf, ns_tok, 1)
        pad_q = lambda a: jnp.pad(a.reshape(db, tq, aw), ((0, 0), (0, qp - tq), (0, 0)))
        q_pad, k_pad, v_pad = pad_q(qs), pad_q(ks), pad_q(vs)
        kmean = _cache_block_means(cache_k, l, page_table, n_past_blocks)
        sel = _select_blocks(q_pad, kmean.reshape(db, n_past_blocks, n_heads, dh), n_heads)
        sel_flat = sel[:, :, :MOBA_TOPK].reshape(db, n_heads * qp * MOBA_TOPK)
        att_s = _sample_attention(q_pad, k_pad, v_pad, sel_flat, page_table, cache_k, cache_v, l, n_heads, tq)
        att_s = att_s[:, :tq].reshape(ns_tok, aw).astype(BF16)
        gys, srs, sis = _s5(xss.reshape(db, tq, sw), state_ssm_re[l].reshape(db, n_state),
                            state_ssm_im[l].reshape(db, n_state), abar_re_row, abar_im_row,
                            bre_bd, bim_bd, cre_bd, cim_bd, d_row, tq)
        merged_s = _merge(att_s, gys.reshape(ns_tok, sw), gas, gbs, wbr, wv, wg, ns_tok)
        hs_mid, u2s, tis, gts = _outproj(merged_s, hs, mod_s[2], mod_s[4], mod_s[3], n2g, wo, wr_t, br_col,
                                         ns_tok, 1)
        outs["ks"].append(ks.reshape(db, tq, n_heads, dh))
        outs["vs"].append(vs.reshape(db, tq, n_heads, dh))
        outs["srs"].append(srs.reshape(db, n_groups, n_st))
        outs["sis"].append(sis.reshape(db, n_groups, n_st))

        n_pad = -(-n_tok // RANK_TM) * RANK_TM
        top_i = jnp.concatenate([tip, tis, jnp.full((MOE_TOPK, n_pad - n_tok), -1, I32)], axis=1)
        rank, counts = _expert_ranks(top_i, n_exp)
        padded = (counts + MOE_SUB - 1) // MOE_SUB * MOE_SUB
        pad_end = jnp.cumsum(padded)
        pad_start = pad_end - padded
        is_exp = top_i[:, :n_tok, None] == jnp.arange(n_exp, dtype=I32)
        dest = jnp.sum(jnp.where(is_exp, pad_start, 0), axis=-1) + rank[:, :n_tok]
        n_sub_max = -(-(n_tok * MOE_TOPK) // MOE_SUB) + n_exp
        n_rows = n_sub_max * MOE_SUB
        assert n_tok < (1 << (32 - SLOT_BITS)) and n_rows <= (1 << SLOT_BITS)
        tok_ids = jnp.broadcast_to(jnp.arange(n_tok, dtype=jnp.uint32), (MOE_TOPK, n_tok))
        real = (tok_ids << SLOT_BITS) | (tok_ids + jnp.arange(MOE_TOPK, dtype=jnp.uint32)[:, None] * n_tok)
        empty = jnp.uint32(0xFFFFFFFF)
        desc = jnp.full((n_rows,), empty, jnp.uint32).at[dest.reshape(-1)].set(real.reshape(-1))
        is_empty = desc == empty
        spare = (n_tok * MOE_TOPK - 1 + jnp.cumsum(is_empty.astype(I32))).astype(jnp.uint32)
        desc = jnp.where(is_empty, spare, desc)
        packed = lax.bitcast_convert_type(jnp.concatenate([desc, jnp.zeros((MOE_SUB,), jnp.uint32)]), I32)
        used_sub = pad_end[-1:] // MOE_SUB
        sub_start = jnp.concatenate([pad_start // MOE_SUB, used_sub]).astype(I32)
        n_sub = jnp.concatenate([padded // MOE_SUB, n_sub_max - used_sub]).astype(I32)

        u_all = jnp.concatenate([u2p, u2s], axis=0)
        slots = _experts(u_all, sub_start, n_sub, packed, _prep_gate_up(w_gate_up[l]), w_down[l],
                         _group_bias(b_gate_up[l]), b_down[l][:, None, :], n_rows)

        gates_t = jnp.concatenate([gtp, gts], axis=1).T
        last = l == depth - 1
        fg = final_g[None] if last else None
        assert last, "depth > 1 needs the un-normalised residual stream"
        hp = _combine(slots, n_tok, 0, gates_t[:np_tok], hp_mid, mod_p[5], fg, COMBINE_TM, t // COMBINE_TM)
        hs = _combine(slots, n_tok, np_tok, gates_t[np_tok:], hs_mid, mod_s[5], fg, ns_tok, 1)

    st = lambda name: jnp.stack(outs[name])
    return (hp.reshape(bp, t, d), hs.reshape(db, tq, d),
            st("kp"), st("vp"), st("srp"), st("sip"), st("ks"), st("vs"), st("srs"), st("sis"))
```
